```python
import math
import jax, jax.numpy as jnp
from jax import lax
import numpy as np

D_MODEL = 1024
BATCH = 4
SEQ = 4096
DEPTH = 1

N_META = 16
CONV_W = D_MODEL // 2
CONV_GROUPS = 8
LRU_W = D_MODEL // 2
LRU_HEADS = 8
D_MIX = CONV_W + LRU_W
D_IN = 2 * CONV_W + 2 * LRU_W
CONV_K = 31
LRU_CONV_K = 4
RG_C = 8.0
N_EXPERTS = 32
TOP_K = 4
D_EXPERT = D_MODEL
SWIGLU_ALPHA = 1.702
SWIGLU_LIMIT = 7.0
NORM_EPS = 1e-6
LN_EPS = 1e-5

kernel_name = "hymba_conformer_rglru_moe_block"


def rms_norm(x, g):
    x32 = x.astype(jnp.float32)
    y = x32 * lax.rsqrt(jnp.mean(x32 * x32, axis=-1, keepdims=True) + NORM_EPS)
    return (y * g.astype(jnp.float32)).astype(x.dtype)


def layer_norm(x, g, b):
    x32 = x.astype(jnp.float32)
    mu = jnp.mean(x32, axis=-1, keepdims=True)
    xc = x32 - mu
    y = xc * lax.rsqrt(jnp.mean(xc * xc, axis=-1, keepdims=True) + LN_EPS)
    return (y * g.astype(jnp.float32) + b.astype(jnp.float32)).astype(x.dtype)


def causal_depthwise_conv(x, w, b):
    k = w.shape[0]
    xp = jnp.pad(x, ((0, 0), (k - 1, 0), (0, 0)))
    y = lax.conv_general_dilated(
        xp, w[:, None, :].astype(x.dtype), window_strides=(1,), padding="VALID",
        dimension_numbers=("NWC", "WIO", "NWC"), feature_group_count=x.shape[-1])
    return y + b.astype(x.dtype)


def conformer_conv_group(val, gate, dw_w, dw_b, ln_g, ln_b):
    u = val * jax.nn.sigmoid(gate)
    u = causal_depthwise_conv(u, dw_w, dw_b)
    u = layer_norm(u, ln_g, ln_b)
    return jax.nn.silu(u)


def _lru_combine(left, right):
    a1, b1 = left
    a2, b2 = right
    return a1 * a2, a2 * b1 + b2


def rg_lru(x, w_a, b_a, w_x, b_x, lam):
    bsz, t, c = x.shape
    xh = x.reshape(bsz, t, LRU_HEADS, c // LRU_HEADS)
    r = jax.nn.sigmoid(jnp.einsum("bthi,hij->bthj", xh, w_a).reshape(bsz, t, c) + b_a)
    i = jax.nn.sigmoid(jnp.einsum("bthi,hij->bthj", xh, w_x).reshape(bsz, t, c) + b_x)
    r32 = r.astype(jnp.float32)
    log_a = -RG_C * r32 * jax.nn.softplus(-lam.astype(jnp.float32))
    a = jnp.exp(log_a)
    mult = jnp.sqrt(-jnp.expm1(2.0 * log_a))
    is_start = (jnp.arange(t) == 0)[None, :, None]
    mult = jnp.where(is_start, 1.0, mult)
    bterm = mult * (i.astype(jnp.float32) * x.astype(jnp.float32))
    _, h = lax.associative_scan(_lru_combine, (a, bterm), axis=1)
    return h.astype(x.dtype)


def griffin_group(xin, gin, conv_w, conv_b, w_a, b_a, w_x, b_x, lam):
    y = jax.nn.gelu(gin, approximate=True)
    u = causal_depthwise_conv(xin, conv_w, conv_b)
    u = rg_lru(u, w_a, b_a, w_x, b_x, lam)
    return u * y


def moe_ffn(u, w_router, b_router, w_gu, b_gu, w_down, b_down):
    bsz, t, d = u.shape
    tok = u.reshape(bsz * t, d)
    logits = (tok @ w_router + b_router).astype(jnp.float32)
    vals, idx = lax.top_k(logits, TOP_K)
    wts = jax.nn.softmax(vals, axis=-1)
    gates = jnp.einsum("nk,nke->ne", wts, jax.nn.one_hot(idx, N_EXPERTS, dtype=jnp.float32))
    gates = gates.astype(tok.dtype)

    def expert_step(acc, xs):
        wgu, bgu, wd, bd, g = xs
        hgu = tok @ wgu + bgu
        x_glu = jnp.minimum(hgu[:, 0::2], SWIGLU_LIMIT)
        x_lin = jnp.clip(hgu[:, 1::2], -SWIGLU_LIMIT, SWIGLU_LIMIT)
        act = x_glu * jax.nn.sigmoid(SWIGLU_ALPHA * x_glu) * (x_lin + 1.0)
        out = act @ wd + bd
        return acc + g[:, None] * out, None

    acc, _ = lax.scan(expert_step, jnp.zeros_like(tok), (w_gu, b_gu, w_down, b_down, gates.T))
    return acc.reshape(bsz, t, d)


def setup_inputs(seed: int = 0) -> dict:
    key = jax.random.key(seed)
    ks = jax.random.split(key, 24)
    f32 = jnp.float32
    L = DEPTH
    nrm = lambda k, shape, fan_in: jax.random.normal(k, shape, f32) * (fan_in ** -0.5)
    small = lambda k, shape, s=0.02: jax.random.normal(k, shape, f32) * s
    blk = LRU_W // LRU_HEADS
    u = jax.random.uniform(ks[14], (L, LRU_W), f32, 0.9, 0.999)
    a0 = u ** (1.0 / RG_C)
    lam = jnp.log(a0) - jnp.log1p(-a0)
    return {
        "x": jax.random.normal(ks[0], (BATCH, SEQ, D_MODEL), f32),
        "meta_tokens": jax.random.normal(ks[1], (N_META, D_MODEL), f32),
        "g_mix": 1.0 + small(ks[2], (L, D_MODEL)),
        "w_in": nrm(ks[3], (L, D_MODEL, D_IN), D_MODEL),
        "conv_dw_w": nrm(ks[4], (L, CONV_K, CONV_W), CONV_K),
        "conv_dw_b": small(ks[5], (L, CONV_W)),
        "conv_ln_g": 1.0 + small(ks[6], (L, CONV_W)),
        "conv_ln_b": small(ks[7], (L, CONV_W)),
        "lru_conv_w": nrm(ks[8], (L, LRU_CONV_K, LRU_W), LRU_CONV_K),
        "lru_conv_b": small(ks[9], (L, LRU_W)),
        "lru_wa": nrm(ks[10], (L, LRU_HEADS, blk, blk), blk),
        "lru_ba": small(ks[11], (L, LRU_W)),
        "lru_wx": nrm(ks[12], (L, LRU_HEADS, blk, blk), blk),
        "lru_bx": small(ks[13], (L, LRU_W)),
        "lru_lambda": lam,
        "w_out": nrm(ks[15], (L, D_MIX, D_MODEL), D_MIX),
        "g_ffn": 1.0 + small(ks[16], (L, D_MODEL)),
        "w_router": nrm(ks[17], (L, D_MODEL, N_EXPERTS), D_MODEL),
        "b_router": small(ks[18], (L, N_EXPERTS), 0.01),
        "w_gate_up": nrm(ks[19], (L, N_EXPERTS, D_MODEL, 2 * D_EXPERT), D_MODEL),
        "b_gate_up": small(ks[20], (L, N_EXPERTS, 2 * D_EXPERT)),
        "w_down": nrm(ks[21], (L, N_EXPERTS, D_EXPERT, D_MODEL), D_EXPERT),
        "b_down": small(ks[22], (L, N_EXPERTS, D_MODEL)),
        "g_final": 1.0 + small(ks[23], (D_MODEL,)),
    }


def reference(x, meta_tokens, g_mix, w_in, conv_dw_w, conv_dw_b, conv_ln_g, conv_ln_b,
              lru_conv_w, lru_conv_b, lru_wa, lru_ba, lru_wx, lru_bx, lru_lambda, w_out,
              g_ffn, w_router, b_router, w_gate_up, b_gate_up, w_down, b_down, g_final):
    bsz = x.shape[0]
    meta = jnp.broadcast_to(meta_tokens[None].astype(x.dtype), (bsz, N_META, D_MODEL))
    h = jnp.concatenate([meta, x], axis=1)
    for l in range(DEPTH):
        u = rms_norm(h, g_mix[l])
        z = u @ w_in[l]
        c_val, c_gate, r_x, r_gate = jnp.split(
            z, [CONV_W, 2 * CONV_W, 2 * CONV_W + LRU_W], axis=-1)
        conv_out = conformer_conv_group(c_val, c_gate, conv_dw_w[l], conv_dw_b[l],
                                        conv_ln_g[l], conv_ln_b[l])
        lru_out = griffin_group(r_x, r_gate, lru_conv_w[l], lru_conv_b[l], lru_wa[l], lru_ba[l],
                                lru_wx[l], lru_bx[l], lru_lambda[l])
        mix = jnp.concatenate([conv_out, lru_out], axis=-1)
        h = h + mix @ w_out[l]
        u = rms_norm(h, g_ffn[l])
        h = h + moe_ffn(u, w_router[l], b_router[l], w_gate_up[l], b_gate_up[l],
                        w_down[l], b_down[l])
    h = rms_norm(h, g_final)
    return h[:, N_META:]
```

```python
import functools
import math

import jax
import jax.numpy as jnp
from jax import lax
from jax.experimental import pallas as pl
from jax.experimental.pallas import tpu as pltpu

D_MODEL = 1024
N_META = 16
CONV_W = 512
LRU_W = 512
LRU_HEADS = 8
CONV_K = 31
LRU_CONV_K = 4
RG_C = 8.0
N_EXPERTS = 32
TOP_K = 4
SWIGLU_ALPHA = 1.702
SWIGLU_LIMIT = 7.0
NORM_EPS = 1e-6
LN_EPS = 1e-5

SUBLANES = 8
LANES = 128
MXU_DIM = 256
VMEM_LIMIT = 58 * 1024 * 1024

MIX_ROWS = 512
CONV_HALO = 32
LRU_HALO = 8
MOE_ROWS = 512

BF16 = jnp.bfloat16
F32 = jnp.float32


def _dot(a, b):
    return jnp.dot(a, b, preferred_element_type=F32)


def _rms(x, g):
    return x * lax.rsqrt(jnp.mean(x * x, axis=-1, keepdims=True) + NORM_EPS) * g


def _mix_rows(h, is_seq_start, p, s):
    rows = h.shape[0]
    u = _rms(h, p["g_mix"][...])
    z = _dot(u.astype(BF16), p["w_in"][...])
    c_val, c_gate = z[:, :CONV_W], z[:, CONV_W:2 * CONV_W]
    r_x, r_gate = z[:, 2 * CONV_W:2 * CONV_W + LRU_W], z[:, 2 * CONV_W + LRU_W:]

    gbuf = s["gbuf"]
    gbuf[CONV_HALO:CONV_HALO + rows, :] = c_val * jax.nn.sigmoid(c_gate)
    acc = jnp.broadcast_to(p["dw_b"][...], (rows, CONV_W))
    for k in range(CONV_K):
        acc = acc + p["dw_w"][k:k + 1, :] * gbuf[CONV_HALO - (CONV_K - 1) + k:CONV_HALO - (CONV_K - 1) + k + rows, :]
    gbuf[0:CONV_HALO, :] = gbuf[rows:rows + CONV_HALO, :]
    mu = jnp.mean(acc, axis=-1, keepdims=True)
    xc = acc - mu
    y = xc * lax.rsqrt(jnp.mean(xc * xc, axis=-1, keepdims=True) + LN_EPS)
    y = y * p["ln_g"][...] + p["ln_b"][...]
    conv_out = y * jax.nn.sigmoid(y)

    rbuf = s["rbuf"]
    rbuf[LRU_HALO:LRU_HALO + rows, :] = r_x
    xr = jnp.broadcast_to(p["lc_b"][...], (rows, LRU_W))
    for k in range(LRU_CONV_K):
        xr = xr + p["lc_w"][k:k + 1, :] * rbuf[LRU_HALO - (LRU_CONV_K - 1) + k:LRU_HALO - (LRU_CONV_K - 1) + k + rows, :]
    rbuf[0:LRU_HALO, :] = rbuf[rows:rows + LRU_HALO, :]
    ax = _dot(xr.astype(BF16), p["w_ax"][...]) + p["b_ax"][...]
    r = jax.nn.sigmoid(ax[:, :LRU_W])
    i = jax.nn.sigmoid(ax[:, LRU_W:])
    log_a = (-RG_C) * r * jax.nn.softplus(-p["lam"][...])
    a = jnp.exp(log_a)
    t = jnp.tanh(log_a)
    mult = jnp.sqrt(-2.0 * t / (1.0 - t))
    if is_seq_start:
        row = lax.broadcasted_iota(jnp.int32, (rows, LRU_W), 0)
        mult = jnp.where(row == 0, 1.0, mult)
    b = mult * (i * xr)

    sub = lax.broadcasted_iota(jnp.int32, (rows, LRU_W), 0) % SUBLANES
    for d in (1, 2, 4):
        keep = sub >= d
        a_sh = pltpu.roll(a, d, 0)
        b_sh = pltpu.roll(b, d, 0)
        b = jnp.where(keep, a * b_sh + b, b)
        a = jnp.where(keep, a * a_sh, a)
    s["abuf"][0:rows, :] = a
    s["bbuf"][0:rows, :] = b

    def carry_step(j, hprev):
        off = pl.multiple_of(j * SUBLANES, SUBLANES)
        hj = s["abuf"][pl.ds(off, SUBLANES), :] * hprev + s["bbuf"][pl.ds(off, SUBLANES), :]
        s["bbuf"][pl.ds(off, SUBLANES), :] = hj
        return hj[SUBLANES - 1:SUBLANES, :]

    s["hcar"][...] = lax.fori_loop(0, rows // SUBLANES, carry_step, s["hcar"][...])
    lru_out = s["bbuf"][0:rows, :] * jax.nn.gelu(r_gate, approximate=True)
    return conv_out, lru_out


def _topk_gates(logits):
    rows, n_e = logits.shape
    lane = lax.broadcasted_iota(jnp.int32, (rows, n_e), 1).astype(F32)
    masked = logits
    vals, sels = [], []
    for _ in range(TOP_K):
        m = jnp.max(masked, axis=-1, keepdims=True)
        first = jnp.min(jnp.where(masked == m, lane, float(n_e)), axis=-1, keepdims=True)
        sel = lane == first
        vals.append(m)
        sels.append(sel)
        masked = jnp.where(sel, -jnp.inf, masked)
    exps = [jnp.exp(v - vals[0]) for v in vals]
    denom = exps[0] + exps[1] + exps[2] + exps[3]
    gates = jnp.zeros_like(logits)
    for sel, e in zip(sels, exps):
        gates = jnp.where(sel, e / denom, gates)
    return gates


_MIX_PARAMS = ("g_mix", "w_in", "dw_w", "dw_b", "ln_g", "ln_b", "lc_w", "lc_b", "w_ax", "b_ax", "lam",
               "w_out", "g_ffn", "w_r", "b_r")
_MIX_SCRATCH = ("gbuf", "rbuf", "abuf", "bbuf", "hcar")


def _mixer_kernel(x_ref, meta_ref, *refs):
    n_p = len(_MIX_PARAMS)
    p = dict(zip(_MIX_PARAMS, refs[:n_p]))
    h1_ref, u2_ref, gates_ref = refs[n_p:n_p + 3]
    s = dict(zip(_MIX_SCRATCH, refs[n_p + 3:]))

    @pl.when(pl.program_id(1) == 0)
    def _():
        s["gbuf"][0:CONV_HALO, :] = jnp.zeros((CONV_HALO, CONV_W), F32)
        s["rbuf"][0:LRU_HALO, :] = jnp.zeros((LRU_HALO, LRU_W), F32)
        s["hcar"][...] = jnp.zeros((1, LRU_W), F32)
        _mix_rows(meta_ref[...], True, p, s)

    h = x_ref[0]
    conv_out, lru_out = _mix_rows(h, False, p, s)
    w_out = p["w_out"]
    h1 = h + _dot(conv_out.astype(BF16), w_out[0:CONV_W, :]) + _dot(lru_out.astype(BF16), w_out[CONV_W:, :])
    h1_ref[...] = h1
    u2 = _rms(h1, p["g_ffn"][...]).astype(BF16)
    u2_ref[...] = u2
    logits = _dot(u2, p["w_r"][...]) + p["b_r"][...]
    gates_ref[...] = _topk_gates(logits)


def _mixer(x, meta, params):
    bsz, seq, d = x.shape
    n_chunks = seq // MIX_ROWS
    n_tok = bsz * seq

    def const(a):
        return pl.BlockSpec(a.shape, lambda b, c: (0,) * a.ndim)

    row_block = lambda width: pl.BlockSpec((MIX_ROWS, width), lambda b, c: (b * n_chunks + c, 0))
    return pl.pallas_call(
        _mixer_kernel,
        grid=(bsz, n_chunks),
        in_specs=[pl.BlockSpec((1, MIX_ROWS, d), lambda b, c: (b, c, 0)), const(meta)]
        + [const(params[k]) for k in _MIX_PARAMS],
        out_specs=[row_block(d), row_block(d), row_block(N_EXPERTS)],
        out_shape=[jax.ShapeDtypeStruct((n_tok, d), F32), jax.ShapeDtypeStruct((n_tok, d), BF16),
                   jax.ShapeDtypeStruct((n_tok, N_EXPERTS), F32)],
        scratch_shapes=[pltpu.VMEM((CONV_HALO + MIX_ROWS, CONV_W), F32), pltpu.VMEM((LRU_HALO + MIX_ROWS, LRU_W), F32),
                        pltpu.VMEM((MIX_ROWS, LRU_W), F32), pltpu.VMEM((MIX_ROWS, LRU_W), F32),
                        pltpu.VMEM((1, LRU_W), F32)],
        compiler_params=pltpu.CompilerParams(dimension_semantics=("arbitrary", "arbitrary"),
                                             vmem_limit_bytes=VMEM_LIMIT),
        name="mixer",
    )(x, meta, *[params[k] for k in _MIX_PARAMS])


def _deinterleave_perm():
    src = lax.broadcasted_iota(jnp.int32, (MXU_DIM, MXU_DIM), 0)
    dst = lax.broadcasted_iota(jnp.int32, (MXU_DIM, MXU_DIM), 1)
    want = jnp.where(dst < LANES, 2 * dst, 2 * (dst - LANES) + 1)
    return jnp.where(src == want, 1.0, 0.0).astype(BF16)


def _load_expert_weights(wgu_ref, wg_s, wl_s):
    perm = _deinterleave_perm()
    d_ff = wg_s.shape[1]
    for c in range(2 * d_ff // MXU_DIM):
        chunk = wgu_ref[0, :, c * MXU_DIM:(c + 1) * MXU_DIM].astype(BF16)
        o = _dot(chunk, perm)
        wg_s[:, c * LANES:(c + 1) * LANES] = o[:, :LANES].astype(BF16)
        wl_s[:, c * LANES:(c + 1) * LANES] = o[:, LANES:].astype(BF16)


def _expert_mlp(x, wg, wl, bg, bl, wd, bd):
    x_glu = jnp.minimum(_dot(x, wg) + bg, SWIGLU_LIMIT)
    x_lin = jnp.clip(_dot(x, wl) + bl, -SWIGLU_LIMIT, SWIGLU_LIMIT)
    act = x_glu * jax.nn.sigmoid(SWIGLU_ALPHA * x_glu) * (x_lin + 1.0)
    return _dot(act.astype(BF16), wd) + bd


def _moe_dense_kernel(x_ref, gates_ref, h1_ref, wgu_ref, bg_ref, bl_ref, wd_ref, bd_ref, gfin_ref,
                      out_ref, acc_ref, wg_s, wl_s):
    e = pl.program_id(1)

    @pl.when(e == 0)
    def _():
        acc_ref[...] = h1_ref[...]

    _load_expert_weights(wgu_ref, wg_s, wl_s)
    y = _expert_mlp(x_ref[...], wg_s[...], wl_s[...], bg_ref[0], bl_ref[0], wd_ref[0].astype(BF16), bd_ref[0])
    gates = gates_ref[...]
    lane = lax.broadcasted_iota(jnp.int32, gates.shape, 1)
    g_e = jnp.sum(jnp.where(lane == e, gates, 0.0), axis=-1, keepdims=True)
    acc_ref[...] += g_e * y

    @pl.when(e == pl.num_programs(1) - 1)
    def _():
        out_ref[...] = _rms(acc_ref[...], gfin_ref[...])


def _moe_dense(u2, gates, h1, w_gu, b_g, b_l, w_d, b_d, g_final):
    n_tok, d = u2.shape
    n_e, _, d_ff2 = w_gu.shape
    d_ff = d_ff2 // 2
    rows = lambda width: pl.BlockSpec((MOE_ROWS, width), lambda i, e: (i, 0))
    per_e = lambda a: pl.BlockSpec((1,) + a.shape[1:], lambda i, e: (e,) + (0,) * (a.ndim - 1))
    return pl.pallas_call(
        _moe_dense_kernel,
        grid=(n_tok // MOE_ROWS, n_e),
        in_specs=[rows(d), rows(n_e), rows(d), per_e(w_gu), per_e(b_g), per_e(b_l), per_e(w_d), per_e(b_d),
                  pl.BlockSpec(g_final.shape, lambda i, e: (0, 0))],
        out_specs=rows(d),
        out_shape=jax.ShapeDtypeStruct((n_tok, d), F32),
        scratch_shapes=[pltpu.VMEM((MOE_ROWS, d), F32), pltpu.VMEM((d, d_ff), BF16), pltpu.VMEM((d, d_ff), BF16)],
        compiler_params=pltpu.CompilerParams(dimension_semantics=("arbitrary", "arbitrary"),
                                             vmem_limit_bytes=VMEM_LIMIT),
        name="moe_dense",
    )(u2, gates, h1, w_gu, b_g, b_l, w_d, b_d, g_final)


def _block_diag(w):
    heads, n, _ = w.shape
    eye = jnp.eye(heads, dtype=w.dtype)
    return (eye[:, None, :, None] * w[:, :, None, :]).reshape(heads * n, heads * n)


def kernel(x, meta_tokens, g_mix, w_in, conv_dw_w, conv_dw_b, conv_ln_g, conv_ln_b, lru_conv_w, lru_conv_b,
           lru_wa, lru_ba, lru_wx, lru_bx, lru_lambda, w_out, g_ffn, w_router, b_router, w_gate_up, b_gate_up,
           w_down, b_down, g_final):
    bsz, seq, d = x.shape
    assert w_in.shape[0] == 1 and seq % MIX_ROWS == 0 and (bsz * seq) % MOE_ROWS == 0
    row = lambda v: v.reshape(1, -1)
    params = {
        "g_mix": row(g_mix[0]), "w_in": w_in[0].astype(BF16),
        "dw_w": conv_dw_w[0], "dw_b": row(conv_dw_b[0]), "ln_g": row(conv_ln_g[0]), "ln_b": row(conv_ln_b[0]),
        "lc_w": lru_conv_w[0], "lc_b": row(lru_conv_b[0]),
        "w_ax": jnp.concatenate([_block_diag(lru_wa[0]), _block_diag(lru_wx[0])], axis=1).astype(BF16),
        "b_ax": row(jnp.concatenate([lru_ba[0], lru_bx[0]])), "lam": row(lru_lambda[0]),
        "w_out": w_out[0].astype(BF16), "g_ffn": row(g_ffn[0]),
        "w_r": w_router[0].astype(BF16), "b_r": row(b_router[0]),
    }
    h1, u2, gates = _mixer(x, meta_tokens, params)
    n_e = w_gate_up.shape[1]
    b_gu = b_gate_up[0].reshape(n_e, 1, -1, 2)
    out = _moe_dense(u2, gates, h1, w_gate_up[0], b_gu[..., 0], b_gu[..., 1], w_down[0],
                     b_down[0].reshape(n_e, 1, -1), row(g_final))
    return out.reshape(bsz, seq, d)
```

```python
import functools

import jax
import jax.numpy as jnp
from jax import lax
from jax.experimental import pallas as pl
from jax.experimental.pallas import tpu as pltpu

D_MODEL = 1024
N_META = 16
CONV_W = 512
LRU_W = 512
LRU_HEADS = 8
CONV_K = 31
LRU_CONV_K = 4
RG_C = 8.0
N_EXPERTS = 32
TOP_K = 4
SWIGLU_ALPHA = 1.702
SWIGLU_LIMIT = 7.0
NORM_EPS = 1e-6
LN_EPS = 1e-5

SUBLANES = 8
LANES = 128
MXU_DIM = 256
VMEM_LIMIT = 58 * 1024 * 1024

BLK = 512
CONV_HALO = 32
LRU_HALO = 8
PIECE = 16
CHUNK = 256
SORT_ROWS = -(-(TOP_K * BLK + N_EXPERTS * (PIECE - 1)) // CHUNK) * CHUNK
N_PIECES = SORT_ROWS // PIECE
TILE = 512
TAIL_PIECES = TILE // PIECE - 1

BF16 = jnp.bfloat16
F32 = jnp.float32


def _dot(a, b):
    return jnp.dot(a, b, preferred_element_type=F32)


def _rms(x, g):
    return x * lax.rsqrt(jnp.mean(x * x, axis=-1, keepdims=True) + NORM_EPS) * g


def _mix_rows(h, is_seq_start, p, s):
    rows = h.shape[0]
    u = _rms(h, p["g_mix"][...])
    z = _dot(u.astype(BF16), p["w_in"][...])
    c_val, c_gate = z[:, :CONV_W], z[:, CONV_W:2 * CONV_W]
    r_x, r_gate = z[:, 2 * CONV_W:2 * CONV_W + LRU_W], z[:, 2 * CONV_W + LRU_W:]

    gbuf = s["gbuf"]
    gbuf[CONV_HALO:CONV_HALO + rows, :] = c_val * jax.nn.sigmoid(c_gate)
    acc = jnp.broadcast_to(p["dw_b"][...], (rows, CONV_W))
    for k in range(CONV_K):
        acc = acc + p["dw_w"][k:k + 1, :] * gbuf[CONV_HALO - (CONV_K - 1) + k:CONV_HALO - (CONV_K - 1) + k + rows, :]
    gbuf[0:CONV_HALO, :] = gbuf[rows:rows + CONV_HALO, :]
    mu = jnp.mean(acc, axis=-1, keepdims=True)
    xc = acc - mu
    y = xc * lax.rsqrt(jnp.mean(xc * xc, axis=-1, keepdims=True) + LN_EPS)
    y = y * p["ln_g"][...] + p["ln_b"][...]
    conv_out = y * jax.nn.sigmoid(y)

    rbuf = s["rbuf"]
    rbuf[LRU_HALO:LRU_HALO + rows, :] = r_x
    xr = jnp.broadcast_to(p["lc_b"][...], (rows, LRU_W))
    for k in range(LRU_CONV_K):
        xr = xr + p["lc_w"][k:k + 1, :] * rbuf[LRU_HALO - (LRU_CONV_K - 1) + k:LRU_HALO - (LRU_CONV_K - 1) + k + rows, :]
    rbuf[0:LRU_HALO, :] = rbuf[rows:rows + LRU_HALO, :]
    ax = _dot(xr.astype(BF16), p["w_ax"][...]) + p["b_ax"][...]
    r = jax.nn.sigmoid(ax[:, :LRU_W])
    i = jax.nn.sigmoid(ax[:, LRU_W:])
    log_a = (-RG_C) * r * jax.nn.softplus(-p["lam"][...])
    a = jnp.exp(log_a)
    t = jnp.tanh(log_a)
    mult = jnp.sqrt(-2.0 * t / (1.0 - t))
    if is_seq_start:
        row = lax.broadcasted_iota(jnp.int32, (rows, LRU_W), 0)
        mult = jnp.where(row == 0, 1.0, mult)
    b = mult * (i * xr)

    sub = lax.broadcasted_iota(jnp.int32, (rows, LRU_W), 0) % SUBLANES
    for d in (1, 2, 4):
        keep = sub >= d
        a_sh = pltpu.roll(a, d, 0)
        b_sh = pltpu.roll(b, d, 0)
        b = jnp.where(keep, a * b_sh + b, b)
        a = jnp.where(keep, a * a_sh, a)
    s["abuf"][0:rows, :] = a
    s["bbuf"][0:rows, :] = b

    def carry_step(j, hprev):
        off = pl.multiple_of(j * SUBLANES, SUBLANES)
        hj = s["abuf"][pl.ds(off, SUBLANES), :] * hprev + s["bbuf"][pl.ds(off, SUBLANES), :]
        s["bbuf"][pl.ds(off, SUBLANES), :] = hj
        return hj[SUBLANES - 1:SUBLANES, :]

    s["hcar"][...] = lax.fori_loop(0, rows // SUBLANES, carry_step, s["hcar"][...])
    lru_out = s["bbuf"][0:rows, :] * jax.nn.gelu(r_gate, approximate=True)
    return conv_out, lru_out


def _route(logits):
    rows, n_e = logits.shape
    lane = lax.broadcasted_iota(jnp.int32, (rows, n_e), 1).astype(F32)
    masked = logits
    vals, sels = [], []
    for _ in range(TOP_K):
        m = jnp.max(masked, axis=-1, keepdims=True)
        first = jnp.min(jnp.where(masked == m, lane, float(n_e)), axis=-1, keepdims=True)
        sel = lane == first
        vals.append(m)
        sels.append(sel)
        masked = jnp.where(sel, -jnp.inf, masked)
    exps = [jnp.exp(v - vals[0]) for v in vals]
    denom = exps[0] + exps[1] + exps[2] + exps[3]

    chosen = jnp.zeros_like(logits)
    for sel in sels:
        chosen = jnp.where(sel, 1.0, chosen)
    earlier = (lax.broadcasted_iota(jnp.int32, (rows, rows), 1) < lax.broadcasted_iota(jnp.int32, (rows, rows), 0))
    rank = _dot(jnp.where(earlier, 1.0, 0.0).astype(BF16), chosen.astype(BF16))
    count = jnp.sum(chosen, axis=0, keepdims=True)
    pieces = jnp.floor((count + (PIECE - 1.0)) * (1.0 / PIECE))
    below = (lax.broadcasted_iota(jnp.int32, (n_e, n_e), 0) < lax.broadcasted_iota(jnp.int32, (n_e, n_e), 1))
    seg_start = _dot(jnp.broadcast_to(pieces, (SUBLANES, n_e)).astype(BF16),
                     jnp.where(below, 1.0, 0.0).astype(BF16))[0:1, :] * float(PIECE)
    pos = seg_start + rank

    mlane = lax.broadcasted_iota(jnp.int32, (rows, LANES), 1)
    meta = jnp.zeros((rows, LANES), F32)
    for k in range(TOP_K):
        pos_k = jnp.sum(jnp.where(sels[k], pos, 0.0), axis=-1, keepdims=True)
        meta = jnp.where(mlane == k, pos_k, meta)
        meta = jnp.where(mlane == TOP_K + k, exps[k] / denom, meta)
    return meta, pieces


_MIX_PARAMS = ("g_mix", "w_in", "dw_w", "dw_b", "ln_g", "ln_b", "lc_w", "lc_b", "w_ax", "b_ax", "lam",
               "w_out", "g_ffn", "w_r", "b_r")
_MIX_SCRATCH = ("gbuf", "rbuf", "abuf", "bbuf", "hcar")


def _mixer_kernel(x_ref, meta_tok_ref, *refs):
    n_p = len(_MIX_PARAMS)
    p = dict(zip(_MIX_PARAMS, refs[:n_p]))
    h1_ref, u2_ref, meta_ref, cnt_ref = refs[n_p:n_p + 4]
    s = dict(zip(_MIX_SCRATCH, refs[n_p + 4:]))

    @pl.when(pl.program_id(1) == 0)
    def _():
        s["gbuf"][0:CONV_HALO, :] = jnp.zeros((CONV_HALO, CONV_W), F32)
        s["rbuf"][0:LRU_HALO, :] = jnp.zeros((LRU_HALO, LRU_W), F32)
        s["hcar"][...] = jnp.zeros((1, LRU_W), F32)
        _mix_rows(meta_tok_ref[...], True, p, s)

    h = x_ref[0]
    conv_out, lru_out = _mix_rows(h, False, p, s)
    w_out = p["w_out"]
    h1 = h + _dot(conv_out.astype(BF16), w_out[0:CONV_W, :]) + _dot(lru_out.astype(BF16), w_out[CONV_W:, :])
    h1_ref[...] = h1
    u2 = _rms(h1, p["g_ffn"][...]).astype(BF16)
    u2_ref[...] = u2
    logits = _dot(u2, p["w_r"][...]) + p["b_r"][...]
    meta, pieces = _route(logits)
    meta_ref[...] = meta
    cnt_ref[0] = pieces


def _mixer(x, meta_tok, params):
    bsz, seq, d = x.shape
    n_chunks = seq // BLK
    n_tok = bsz * seq

    def const(a):
        return pl.BlockSpec(a.shape, lambda b, c: (0,) * a.ndim)

    row_block = lambda width: pl.BlockSpec((BLK, width), lambda b, c: (b * n_chunks + c, 0))
    return pl.pallas_call(
        _mixer_kernel,
        grid=(bsz, n_chunks),
        in_specs=[pl.BlockSpec((1, BLK, d), lambda b, c: (b, c, 0)), const(meta_tok)]
        + [const(params[k]) for k in _MIX_PARAMS],
        out_specs=[row_block(d), row_block(d), row_block(LANES),
                   pl.BlockSpec((1, 1, N_EXPERTS), lambda b, c: (b * n_chunks + c, 0, 0))],
        out_shape=[jax.ShapeDtypeStruct((n_tok, d), F32), jax.ShapeDtypeStruct((n_tok, d), BF16),
                   jax.ShapeDtypeStruct((n_tok, LANES), F32),
                   jax.ShapeDtypeStruct((n_tok // BLK, 1, N_EXPERTS), F32)],
        scratch_shapes=[pltpu.VMEM((CONV_HALO + BLK, CONV_W), F32), pltpu.VMEM((LRU_HALO + BLK, LRU_W), F32),
                        pltpu.VMEM((BLK, LRU_W), F32), pltpu.VMEM((BLK, LRU_W), F32),
                        pltpu.VMEM((1, LRU_W), F32)],
        compiler_params=pltpu.CompilerParams(dimension_semantics=("arbitrary", "arbitrary"),
                                             vmem_limit_bytes=VMEM_LIMIT),
        name="mixer",
    )(x, meta_tok, *[params[k] for k in _MIX_PARAMS])


def _routing_tables(cnt, n_tiles):
    n_blocks, n_e = cnt.shape
    tile_pieces = TILE // PIECE
    per_expert = cnt.sum(0)
    tiles_e = (per_expert + tile_pieces - 1) // tile_pieces
    tile_end = jnp.cumsum(tiles_e)
    tile_start = tile_end - tiles_e
    seg_global = tile_start[None, :] * tile_pieces + jnp.cumsum(cnt, axis=0) - cnt
    seg_end_local = jnp.cumsum(cnt, axis=1)
    seg_local = seg_end_local - cnt
    n_pieces = seg_end_local[:, -1]

    j = jnp.arange(N_PIECES, dtype=jnp.int32)
    e_of_j = jnp.minimum((j[None, :, None] >= seg_end_local[:, None, :]).sum(-1), n_e - 1)
    dst = jnp.take_along_axis(seg_global, e_of_j, 1) + j[None, :] - jnp.take_along_axis(seg_local, e_of_j, 1)
    dst = jnp.where(j[None, :] < n_pieces[:, None], dst * PIECE, 0)

    q = jnp.arange(TAIL_PIECES, dtype=jnp.int32)
    tail = (tile_start + 0)[:, None] * tile_pieces + per_expert[:, None] + q[None, :]
    tail = jnp.where(q[None, :] < (tiles_e * tile_pieces - per_expert)[:, None], tail * PIECE, -1)

    total = tile_end[-1]
    i = jnp.arange(n_tiles, dtype=jnp.int32)
    valid = i < total
    idx = jnp.where(valid, i, total - 1)
    t_exp = jnp.minimum((idx[:, None] >= tile_end[None, :]).sum(-1), n_e - 1)
    t_first = valid & (i == tile_start[t_exp])
    i32 = lambda a: a.astype(jnp.int32)
    return (i32(dst).reshape(-1), i32(n_pieces), i32(tail).reshape(-1), i32(total).reshape(1),
            i32(t_exp), i32(idx), i32(valid), i32(t_first))


def _dispatch_kernel(dst_ref, npc_ref, tail_ref, used_ref, u2_ref, meta_ref, xs_hbm, buf, zbuf, sem, zsem, tsem, *,
                     per_step):
    b = pl.program_id(0)
    n_b = pl.num_programs(0)
    slot = b % 2
    n_tiles = xs_hbm.shape[0] // TILE

    def piece_copy(blk, j, slot_):
        row = pl.multiple_of(dst_ref[blk * N_PIECES + j], PIECE)
        src_row = pl.multiple_of(j * PIECE, PIECE)
        return pltpu.make_async_copy(buf.at[slot_, pl.ds(src_row, PIECE), :], xs_hbm.at[pl.ds(row, PIECE), :],
                                     sem.at[slot_])

    def wait_block(blk, slot_):
        def body(j, c):
            piece_copy(blk, j, slot_).wait()
            return c
        lax.fori_loop(0, npc_ref[blk], body, 0)

    @pl.when(b >= 2)
    def _():
        wait_block(b - 2, slot)

    def unused_tile_copy(t):
        return pltpu.make_async_copy(zbuf, xs_hbm.at[pl.ds(pl.multiple_of(t * TILE, TILE), TILE), :], tsem)

    @pl.when(b == 0)
    def _():
        zbuf[...] = jnp.zeros(zbuf.shape, zbuf.dtype)

        def body(t, c):
            unused_tile_copy(t).start()
            return c
        lax.fori_loop(used_ref[0], n_tiles, body, 0)

    def tail_copy(q):
        row = tail_ref[b * per_step + q]
        return row, pltpu.make_async_copy(zbuf.at[0:PIECE, :],
                                          xs_hbm.at[pl.ds(pl.multiple_of(jnp.maximum(row, 0), PIECE), PIECE), :], zsem)

    for q in range(per_step):
        row, cp = tail_copy(q)
        pl.when(row >= 0)(cp.start)

    pos_t = meta_ref[...].T
    u = u2_ref[...]
    n_rows = npc_ref[b] * PIECE
    for c in range(SORT_ROWS // CHUNK):
        @pl.when(c * CHUNK < n_rows)
        def _(c=c):
            r = (lax.broadcasted_iota(jnp.int32, (CHUNK, BLK), 0) + c * CHUNK).astype(F32)
            onehot = jnp.zeros((CHUNK, BLK), F32)
            for k in range(TOP_K):
                onehot = jnp.where(r == pos_t[k:k + 1, :], 1.0, onehot)
            buf[slot, c * CHUNK:(c + 1) * CHUNK, :] = _dot(onehot.astype(BF16), u).astype(BF16)

    def issue(j, c):
        piece_copy(b, j, slot).start()
        return c
    lax.fori_loop(0, npc_ref[b], issue, 0)

    for q in range(per_step):
        row, cp = tail_copy(q)
        pl.when(row >= 0)(cp.wait)

    @pl.when(b == n_b - 1)
    def _():
        wait_block(b, slot)

        @pl.when(n_b >= 2)
        def _():
            wait_block(b - 1, 1 - slot)

        def body(t, c):
            unused_tile_copy(t).wait()
            return c
        lax.fori_loop(used_ref[0], n_tiles, body, 0)


def _dispatch(u2, meta, dst, n_pieces, tail, n_used, n_tiles):
    n_tok, d = u2.shape
    n_blocks = n_tok // BLK
    assert tail.shape[0] % n_blocks == 0
    return pl.pallas_call(
        functools.partial(_dispatch_kernel, per_step=tail.shape[0] // n_blocks),
        grid_spec=pltpu.PrefetchScalarGridSpec(
            num_scalar_prefetch=4,
            grid=(n_blocks,),
            in_specs=[pl.BlockSpec((BLK, d), lambda b, *_: (b, 0)), pl.BlockSpec((BLK, LANES), lambda b, *_: (b, 0))],
            out_specs=pl.BlockSpec(memory_space=pl.ANY),
            scratch_shapes=[pltpu.VMEM((2, SORT_ROWS, d), BF16), pltpu.VMEM((TILE, d), BF16),
                            pltpu.SemaphoreType.DMA((2,)), pltpu.SemaphoreType.DMA(()), pltpu.SemaphoreType.DMA(())],
        ),
        out_shape=jax.ShapeDtypeStruct((n_tiles * TILE, d), BF16),
        compiler_params=pltpu.CompilerParams(dimension_semantics=("arbitrary",), vmem_limit_bytes=VMEM_LIMIT),
        name="dispatch",
    )(dst, n_pieces, tail, n_used, u2, meta)


def _deinterleave_perm():
    src = lax.broadcasted_iota(jnp.int32, (MXU_DIM, MXU_DIM), 0)
    dst = lax.broadcasted_iota(jnp.int32, (MXU_DIM, MXU_DIM), 1)
    want = jnp.where(dst < LANES, 2 * dst, 2 * (dst - LANES) + 1)
    return jnp.where(src == want, 1.0, 0.0).astype(BF16)


def _load_expert_weights(wgu_ref, wg_s, wl_s):
    perm = _deinterleave_perm()
    d_ff = wg_s.shape[1]
    for c in range(2 * d_ff // MXU_DIM):
        chunk = wgu_ref[0, :, c * MXU_DIM:(c + 1) * MXU_DIM].astype(BF16)
        o = _dot(chunk, perm)
        wg_s[:, c * LANES:(c + 1) * LANES] = o[:, :LANES].astype(BF16)
        wl_s[:, c * LANES:(c + 1) * LANES] = o[:, LANES:].astype(BF16)


def _expert_mlp(x, wg, wl, bg, bl, wd, bd):
    x_glu = jnp.minimum(_dot(x, wg) + bg, SWIGLU_LIMIT)
    x_lin = jnp.clip(_dot(x, wl) + bl, -SWIGLU_LIMIT, SWIGLU_LIMIT)
    act = x_glu * jax.nn.sigmoid(SWIGLU_ALPHA * x_glu) * (x_lin + 1.0)
    return _dot(act.astype(BF16), wd) + bd


def _experts_kernel(texp_ref, tblk_ref, tvalid_ref, tfirst_ref, x_ref, wgu_ref, bg_ref, bl_ref, wd_ref, bd_ref,
                    y_ref, wg_s, wl_s, wd_s):
    del texp_ref, tblk_ref
    i = pl.program_id(0)

    @pl.when(tfirst_ref[i] == 1)
    def _():
        _load_expert_weights(wgu_ref, wg_s, wl_s)
        wd_s[...] = wd_ref[0].astype(BF16)

    @pl.when(tvalid_ref[i] == 1)
    def _():
        y = _expert_mlp(x_ref[...], wg_s[...], wl_s[...], bg_ref[0], bl_ref[0], wd_s[...], bd_ref[0])
        y_ref[...] = y.astype(BF16)

    @pl.when(tvalid_ref[i] == 0)
    def _():
        y_ref[...] = jnp.zeros(y_ref.shape, y_ref.dtype)


def _experts(xs, t_exp, t_blk, t_valid, t_first, w_gu, b_g, b_l, w_d, b_d):
    n_rows, d = xs.shape
    d_ff = w_gu.shape[2] // 2
    rows = pl.BlockSpec((TILE, d), lambda i, te, tb, tv, tf: (tb[i], 0))
    per_e = lambda a: pl.BlockSpec((1,) + a.shape[1:], lambda i, te, tb, tv, tf: (te[i],) + (0,) * (a.ndim - 1))
    return pl.pallas_call(
        _experts_kernel,
        grid_spec=pltpu.PrefetchScalarGridSpec(
            num_scalar_prefetch=4,
            grid=(n_rows // TILE,),
            in_specs=[rows, per_e(w_gu), per_e(b_g), per_e(b_l), per_e(w_d), per_e(b_d)],
            out_specs=pl.BlockSpec((TILE, d), lambda i, te, tb, tv, tf: (i, 0)),
            scratch_shapes=[pltpu.VMEM((d, d_ff), BF16), pltpu.VMEM((d, d_ff), BF16), pltpu.VMEM((d_ff, d), BF16)],
        ),
        out_shape=jax.ShapeDtypeStruct((n_rows, d), BF16),
        compiler_params=pltpu.CompilerParams(dimension_semantics=("arbitrary",), vmem_limit_bytes=VMEM_LIMIT),
        name="experts",
    )(t_exp, t_blk, t_valid, t_first, xs, w_gu, b_g, b_l, w_d, b_d)


def _combine_kernel(dst_ref, npc_ref, meta_ref, h1_ref, gfin_ref, ys_hbm, out_ref, ybuf, acc_ref, sem):
    b = pl.program_id(0)
    n_b = pl.num_programs(0)
    slot = b % 2

    def piece_copy(blk, j, slot_):
        row = pl.multiple_of(dst_ref[blk * N_PIECES + j], PIECE)
        dst_row = pl.multiple_of(j * PIECE, PIECE)
        return pltpu.make_async_copy(ys_hbm.at[pl.ds(row, PIECE), :], ybuf.at[slot_, pl.ds(dst_row, PIECE), :],
                                     sem.at[slot_])

    def start_block(blk, slot_):
        def body(j, c):
            piece_copy(blk, j, slot_).start()
            return c
        lax.fori_loop(0, npc_ref[blk], body, 0)

    @pl.when(b == 0)
    def _():
        ybuf[...] = jnp.zeros(ybuf.shape, ybuf.dtype)
        start_block(0, 0)

    @pl.when(b + 1 < n_b)
    def _():
        start_block(b + 1, 1 - slot)

    meta = meta_ref[...]
    acc_ref[...] = h1_ref[...]

    def wait_body(j, c):
        piece_copy(b, j, slot).wait()
        return c
    lax.fori_loop(0, npc_ref[b], wait_body, 0)

    n_rows = npc_ref[b] * PIECE
    for c in range(SORT_ROWS // CHUNK):
        @pl.when(c * CHUNK < n_rows)
        def _(c=c):
            r = (lax.broadcasted_iota(jnp.int32, (BLK, CHUNK), 1) + c * CHUNK).astype(F32)
            w = jnp.zeros((BLK, CHUNK), F32)
            for k in range(TOP_K):
                w = jnp.where(r == meta[:, k:k + 1], meta[:, TOP_K + k:TOP_K + k + 1], w)
            acc_ref[...] += _dot(w.astype(BF16), ybuf[slot, c * CHUNK:(c + 1) * CHUNK, :])

    out_ref[...] = _rms(acc_ref[...], gfin_ref[...])


def _combine(ys, meta, h1, g_final, dst, n_pieces):
    n_tok, d = h1.shape
    return pl.pallas_call(
        _combine_kernel,
        grid_spec=pltpu.PrefetchScalarGridSpec(
            num_scalar_prefetch=2,
            grid=(n_tok // BLK,),
            in_specs=[pl.BlockSpec((BLK, LANES), lambda b, *_: (b, 0)), pl.BlockSpec((BLK, d), lambda b, *_: (b, 0)),
                      pl.BlockSpec(g_final.shape, lambda b, *_: (0, 0)), pl.BlockSpec(memory_space=pl.ANY)],
            out_specs=pl.BlockSpec((BLK, d), lambda b, *_: (b, 0)),
            scratch_shapes=[pltpu.VMEM((2, SORT_ROWS, d), BF16), pltpu.VMEM((BLK, d), F32),
                            pltpu.SemaphoreType.DMA((2,))],
        ),
        out_shape=jax.ShapeDtypeStruct((n_tok, d), F32),
        compiler_params=pltpu.CompilerParams(dimension_semantics=("arbitrary",), vmem_limit_bytes=VMEM_LIMIT),
        name="combine",
    )(dst, n_pieces, meta, h1, g_final, ys)


def _block_diag(w):
    heads, n, _ = w.shape
    eye = jnp.eye(heads, dtype=w.dtype)
    return (eye[:, None, :, None] * w[:, :, None, :]).reshape(heads * n, heads * n)


def kernel(x, meta_tokens, g_mix, w_in, conv_dw_w, conv_dw_b, conv_ln_g, conv_ln_b, lru_conv_w, lru_conv_b,
           lru_wa, lru_ba, lru_wx, lru_bx, lru_lambda, w_out, g_ffn, w_router, b_router, w_gate_up, b_gate_up,
           w_down, b_down, g_final):
    bsz, seq, d = x.shape
    n_e = w_gate_up.shape[1]
    assert w_in.shape[0] == 1 and seq % BLK == 0 and n_e == N_EXPERTS
    n_blocks = bsz * seq // BLK
    assert (n_e * TAIL_PIECES) % n_blocks == 0
    row = lambda v: v.reshape(1, -1)
    params = {
        "g_mix": row(g_mix[0]), "w_in": w_in[0].astype(BF16),
        "dw_w": conv_dw_w[0], "dw_b": row(conv_dw_b[0]), "ln_g": row(conv_ln_g[0]), "ln_b": row(conv_ln_b[0]),
        "lc_w": lru_conv_w[0], "lc_b": row(lru_conv_b[0]),
        "w_ax": jnp.concatenate([_block_diag(lru_wa[0]), _block_diag(lru_wx[0])], axis=1).astype(BF16),
        "b_ax": row(jnp.concatenate([lru_ba[0], lru_bx[0]])), "lam": row(lru_lambda[0]),
        "w_out": w_out[0].astype(BF16), "g_ffn": row(g_ffn[0]),
        "w_r": w_router[0].astype(BF16), "b_r": row(b_router[0]),
    }
    h1, u2, meta, cnt = _mixer(x, meta_tokens, params)

    n_tiles = -(-(n_blocks * (TOP_K * BLK + n_e * (PIECE - 1))) // TILE) + n_e
    dst, n_pieces, tail, n_used, t_exp, t_blk, t_valid, t_first = _routing_tables(cnt[:, 0, :].astype(jnp.int32),
                                                                                  n_tiles)
    xs = _dispatch(u2, meta, dst, n_pieces, tail, n_used, n_tiles)
    b_gu = b_gate_up[0].reshape(n_e, 1, -1, 2)
    ys = _experts(xs, t_exp, t_blk, t_valid, t_first, w_gate_up[0], b_gu[..., 0], b_gu[..., 1], w_down[0],
                  b_down[0].reshape(n_e, 1, -1))
    out = _combine(ys, meta, h1, row(g_final), dst, n_pieces)
    return out.reshape(bsz, seq, d)
```

```python
import jax
import jax.numpy as jnp
from jax import lax
from jax.experimental import pallas as pl
from jax.experimental.pallas import tpu as pltpu

D_MODEL = 1024
N_META = 16
CONV_W = 512
LRU_W = 512
LRU_HEADS = 8
CONV_K = 31
LRU_CONV_K = 4
RG_C = 8.0
N_EXPERTS = 32
TOP_K = 4
SWIGLU_ALPHA = 1.702
SWIGLU_LIMIT = 7.0
NORM_EPS = 1e-6
LN_EPS = 1e-5

SUBLANES = 8
LANES = 128
MXU_DIM = 256
VMEM_LIMIT = 58 * 1024 * 1024

BLK = 512
CONV_HALO = 32
LRU_HALO = 8
PIECE = 16
CHUNK = 256
SORT_ROWS = -(-(TOP_K * BLK + N_EXPERTS * (PIECE - 1)) // CHUNK) * CHUNK
N_PIECES = SORT_ROWS // PIECE
TILE = 512
TAIL_PIECES = TILE // PIECE - 1

BF16 = jnp.bfloat16
F32 = jnp.float32


def _dot(a, b):
    return jnp.dot(a, b, preferred_element_type=F32)


def _rms(x, g):
    return x * lax.rsqrt(jnp.mean(x * x, axis=-1, keepdims=True) + NORM_EPS) * g


def _mix_rows(h, is_seq_start, p, s):
    rows = h.shape[0]
    u = _rms(h, p["g_mix"][...])
    z = _dot(u.astype(BF16), p["w_in"][...])
    c_val, c_gate = z[:, :CONV_W], z[:, CONV_W:2 * CONV_W]
    r_x, r_gate = z[:, 2 * CONV_W:2 * CONV_W + LRU_W], z[:, 2 * CONV_W + LRU_W:]

    gbuf = s["gbuf"]
    gbuf[CONV_HALO:CONV_HALO + rows, :] = c_val * jax.nn.sigmoid(c_gate)
    acc = jnp.broadcast_to(p["dw_b"][...], (rows, CONV_W))
    first = CONV_HALO - (CONV_K - 1)
    for phase in range(SUBLANES):
        taps = [k for k in range(CONV_K) if (first + k) % SUBLANES == phase]
        span = max((first + k) // SUBLANES for k in taps) * SUBLANES + rows
        src = gbuf
        if phase:
            src = s["sbuf"]
            src[0:span, :] = gbuf[phase:phase + span, :]
        for k in taps:
            off = (first + k) // SUBLANES * SUBLANES
            acc = acc + p["dw_w"][k:k + 1, :] * src[off:off + rows, :]
    gbuf[0:CONV_HALO, :] = gbuf[rows:rows + CONV_HALO, :]
    mu = jnp.mean(acc, axis=-1, keepdims=True)
    xc = acc - mu
    y = xc * lax.rsqrt(jnp.mean(xc * xc, axis=-1, keepdims=True) + LN_EPS)
    y = y * p["ln_g"][...] + p["ln_b"][...]
    conv_out = y * jax.nn.sigmoid(y)

    rbuf = s["rbuf"]
    rbuf[LRU_HALO:LRU_HALO + rows, :] = r_x
    xr = jnp.broadcast_to(p["lc_b"][...], (rows, LRU_W))
    for k in range(LRU_CONV_K):
        xr = xr + p["lc_w"][k:k + 1, :] * rbuf[LRU_HALO - (LRU_CONV_K - 1) + k:LRU_HALO - (LRU_CONV_K - 1) + k + rows, :]
    rbuf[0:LRU_HALO, :] = rbuf[rows:rows + LRU_HALO, :]
    ax = _dot(xr.astype(BF16), p["w_ax"][...]) + p["b_ax"][...]
    r = jax.nn.sigmoid(ax[:, :LRU_W])
    i = jax.nn.sigmoid(ax[:, LRU_W:])
    log_a = (-RG_C) * r * jax.nn.softplus(-p["lam"][...])
    a = jnp.exp(log_a)
    t = jnp.tanh(log_a)
    mult = jnp.sqrt(-2.0 * t / (1.0 - t))
    if is_seq_start:
        row = lax.broadcasted_iota(jnp.int32, (rows, LRU_W), 0)
        mult = jnp.where(row == 0, 1.0, mult)
    b = mult * (i * xr)

    sub = lax.broadcasted_iota(jnp.int32, (rows, LRU_W), 0) % SUBLANES
    for d in (1, 2, 4):
        keep = sub >= d
        a_sh = pltpu.roll(a, d, 0)
        b_sh = pltpu.roll(b, d, 0)
        b = jnp.where(keep, a * b_sh + b, b)
        a = jnp.where(keep, a * a_sh, a)
    s["abuf"][0:rows, :] = a
    s["bbuf"][0:rows, :] = b

    def carry_step(j, hprev):
        off = pl.multiple_of(j * SUBLANES, SUBLANES)
        hj = s["abuf"][pl.ds(off, SUBLANES), :] * hprev + s["bbuf"][pl.ds(off, SUBLANES), :]
        s["bbuf"][pl.ds(off, SUBLANES), :] = hj
        return hj[SUBLANES - 1:SUBLANES, :]

    s["hcar"][...] = lax.fori_loop(0, rows // SUBLANES, carry_step, s["hcar"][...])
    lru_out = s["bbuf"][0:rows, :] * jax.nn.gelu(r_gate, approximate=True)
    return conv_out, lru_out


def _route(logits):
    rows, n_e = logits.shape
    lane = lax.broadcasted_iota(jnp.int32, (rows, n_e), 1).astype(F32)
    masked = logits
    vals, sels = [], []
    for _ in range(TOP_K):
        m = jnp.max(masked, axis=-1, keepdims=True)
        first = jnp.min(jnp.where(masked == m, lane, float(n_e)), axis=-1, keepdims=True)
        sel = lane == first
        vals.append(m)
        sels.append(sel)
        masked = jnp.where(sel, -jnp.inf, masked)
    exps = [jnp.exp(v - vals[0]) for v in vals]
    denom = exps[0] + exps[1] + exps[2] + exps[3]

    chosen = jnp.zeros_like(logits)
    for sel in sels:
        chosen = jnp.where(sel, 1.0, chosen)
    earlier = (lax.broadcasted_iota(jnp.int32, (rows, rows), 1) < lax.broadcasted_iota(jnp.int32, (rows, rows), 0))
    rank = _dot(jnp.where(earlier, 1.0, 0.0).astype(BF16), chosen.astype(BF16))
    count = jnp.sum(chosen, axis=0, keepdims=True)
    pieces = jnp.floor((count + (PIECE - 1.0)) * (1.0 / PIECE))
    below = (lax.broadcasted_iota(jnp.int32, (n_e, n_e), 0) < lax.broadcasted_iota(jnp.int32, (n_e, n_e), 1))
    seg_start = _dot(jnp.broadcast_to(pieces, (SUBLANES, n_e)).astype(BF16),
                     jnp.where(below, 1.0, 0.0).astype(BF16))[0:1, :] * float(PIECE)
    pos = seg_start + rank

    mlane = lax.broadcasted_iota(jnp.int32, (rows, LANES), 1)
    meta = jnp.zeros((rows, LANES), F32)
    for k in range(TOP_K):
        pos_k = jnp.sum(jnp.where(sels[k], pos, 0.0), axis=-1, keepdims=True)
        meta = jnp.where(mlane == k, pos_k, meta)
        meta = jnp.where(mlane == TOP_K + k, exps[k] / denom, meta)
    return meta, pieces


_MIX_PARAMS = ("g_mix", "w_in", "dw_w", "dw_b", "ln_g", "ln_b", "lc_w", "lc_b", "w_ax", "b_ax", "lam",
               "w_out", "g_ffn", "w_r", "b_r")
_MIX_SCRATCH = ("gbuf", "sbuf", "rbuf", "abuf", "bbuf", "hcar")


def _mixer_kernel(x_ref, meta_tok_ref, *refs):
    n_p = len(_MIX_PARAMS)
    p = dict(zip(_MIX_PARAMS, refs[:n_p]))
    h1_ref, u2_ref, meta_ref, cnt_ref = refs[n_p:n_p + 4]
    s = dict(zip(_MIX_SCRATCH, refs[n_p + 4:]))

    @pl.when(pl.program_id(1) == 0)
    def _():
        s["gbuf"][0:CONV_HALO, :] = jnp.zeros((CONV_HALO, CONV_W), F32)
        s["rbuf"][0:LRU_HALO, :] = jnp.zeros((LRU_HALO, LRU_W), F32)
        s["hcar"][...] = jnp.zeros((1, LRU_W), F32)
        _mix_rows(meta_tok_ref[...], True, p, s)

    h = x_ref[0]
    conv_out, lru_out = _mix_rows(h, False, p, s)
    w_out = p["w_out"]
    h1 = h + _dot(conv_out.astype(BF16), w_out[0:CONV_W, :]) + _dot(lru_out.astype(BF16), w_out[CONV_W:, :])
    h1_ref[...] = h1
    u2 = _rms(h1, p["g_ffn"][...]).astype(BF16)
    u2_ref[...] = u2
    logits = _dot(u2, p["w_r"][...]) + p["b_r"][...]
    meta, pieces = _route(logits)
    meta_ref[...] = meta
    cnt_ref[0] = pieces


def _mixer(x, meta_tok, params):
    bsz, seq, d = x.shape
    n_chunks = seq // BLK
    n_tok = bsz * seq

    def const(a):
        return pl.BlockSpec(a.shape, lambda b, c: (0,) * a.ndim)

    row_block = lambda width: pl.BlockSpec((BLK, width), lambda b, c: (b * n_chunks + c, 0))
    return pl.pallas_call(
        _mixer_kernel,
        grid=(bsz, n_chunks),
        in_specs=[pl.BlockSpec((1, BLK, d), lambda b, c: (b, c, 0)), const(meta_tok)]
        + [const(params[k]) for k in _MIX_PARAMS],
        out_specs=[row_block(d), row_block(d), row_block(LANES),
                   pl.BlockSpec((1, 1, N_EXPERTS), lambda b, c: (b * n_chunks + c, 0, 0))],
        out_shape=[jax.ShapeDtypeStruct((n_tok, d), F32), jax.ShapeDtypeStruct((n_tok, d), BF16),
                   jax.ShapeDtypeStruct((n_tok, LANES), F32),
                   jax.ShapeDtypeStruct((n_tok // BLK, 1, N_EXPERTS), F32)],
        scratch_shapes=[pltpu.VMEM((CONV_HALO + BLK, CONV_W), F32), pltpu.VMEM((CONV_HALO + BLK, CONV_W), F32),
                        pltpu.VMEM((LRU_HALO + BLK, LRU_W), F32),
                        pltpu.VMEM((BLK, LRU_W), F32), pltpu.VMEM((BLK, LRU_W), F32),
                        pltpu.VMEM((1, LRU_W), F32)],
        compiler_params=pltpu.CompilerParams(dimension_semantics=("arbitrary", "arbitrary"),
                                             vmem_limit_bytes=VMEM_LIMIT),
        name="mixer",
    )(x, meta_tok, *[params[k] for k in _MIX_PARAMS])


def _routing_tables(cnt, n_tiles):
    n_blocks, n_e = cnt.shape
    tile_pieces = TILE // PIECE
    per_expert = cnt.sum(0)
    tiles_e = (per_expert + tile_pieces - 1) // tile_pieces
    tile_end = jnp.cumsum(tiles_e)
    tile_start = tile_end - tiles_e
    seg_global = tile_start[None, :] * tile_pieces + jnp.cumsum(cnt, axis=0) - cnt
    seg_end_local = jnp.cumsum(cnt, axis=1)
    seg_local = seg_end_local - cnt
    n_pieces = seg_end_local[:, -1]

    j = jnp.arange(N_PIECES, dtype=jnp.int32)
    shift = seg_global - seg_local
    step = jnp.concatenate([shift[:, :1], shift[:, 1:] - shift[:, :-1]], axis=1)
    dst = j[None, :] + jnp.where(j[None, :, None] >= seg_local[:, None, :], step[:, None, :], 0).sum(-1)
    dst = jnp.where(j[None, :] < n_pieces[:, None], dst * PIECE, 0)

    tail_start = (tile_start * tile_pieces + per_expert) * PIECE
    tail_count = tiles_e * tile_pieces - per_expert

    total = tile_end[-1]
    i = jnp.arange(n_tiles, dtype=jnp.int32)
    valid = i < total
    idx = jnp.where(valid, i, total - 1)
    t_exp = jnp.minimum((idx[:, None] >= tile_end[None, :]).sum(-1), n_e - 1)
    t_first = valid & ((i[:, None] == tile_start[None, :]) & (tiles_e[None, :] > 0)).any(-1)
    e_ids = jnp.arange(n_e, dtype=jnp.int32)
    later = (e_ids[None, :] > t_exp[:, None]) & (tiles_e[None, :] > 0)
    t_next = jnp.where(later, e_ids[None, :], n_e).min(-1)
    t_next = jnp.where(t_next < n_e, t_next, -1)
    i32 = lambda a: a.astype(jnp.int32)
    misc = jnp.stack([total, tail_count.sum()])
    return (i32(dst).reshape(-1), i32(n_pieces), i32(tail_start), i32(tail_count), i32(misc),
            i32(t_exp), i32(idx), i32(valid), i32(t_first), i32(t_next))


def _wait_pieces(count, max_count, hbm_ref, sem):
    bit = 1
    while bit <= max_count:
        @pl.when((count & bit) != 0)
        def _(rows=bit * PIECE):
            pltpu.make_async_copy(hbm_ref.at[0:rows, :], hbm_ref.at[0:rows, :], sem).wait()
        bit *= 2


def _dispatch_kernel(dst_ref, npc_ref, tail_start_ref, tail_count_ref, misc_ref, u2_ref, meta_ref, xs_hbm,
                     buf, zbuf, sem, zsem, tsem):
    b = pl.program_id(0)
    n_b = pl.num_programs(0)
    slot = b % 2
    n_tiles = xs_hbm.shape[0] // TILE
    used_tiles, n_tail = misc_ref[0], misc_ref[1]

    @pl.when(b >= 2)
    def _():
        _wait_pieces(npc_ref[b - 2], N_PIECES, xs_hbm, sem.at[slot])

    def unused_tile_copy(t):
        return pltpu.make_async_copy(zbuf, xs_hbm.at[pl.ds(pl.multiple_of(t * TILE, TILE), TILE), :], tsem)

    @pl.when(b == 0)
    def _():
        zbuf[...] = jnp.zeros(zbuf.shape, zbuf.dtype)

        def tile_body(t, c):
            unused_tile_copy(t).start()
            return c
        lax.fori_loop(used_tiles, n_tiles, tile_body, 0)

        def expert_body(e, c):
            def piece_body(q, c2):
                row = pl.multiple_of(tail_start_ref[e] + q * PIECE, PIECE)
                pltpu.make_async_copy(zbuf.at[0:PIECE, :], xs_hbm.at[pl.ds(row, PIECE), :], zsem).start()
                return c2
            lax.fori_loop(0, tail_count_ref[e], piece_body, 0)
            return c
        lax.fori_loop(0, tail_start_ref.shape[0], expert_body, 0)

    pos_t = meta_ref[...].T
    u = u2_ref[...]

    def sorted_rows(lo, n):
        r = (lax.broadcasted_iota(jnp.int32, (n, BLK), 0) + lo).astype(F32)
        onehot = jnp.zeros((n, BLK), F32)
        for k in range(TOP_K):
            onehot = jnp.where(r == pos_t[k:k + 1, :], 1.0, onehot)
        return _dot(onehot.astype(BF16), u).astype(BF16)

    n_rows = npc_ref[b] * PIECE
    buf[slot, 0:TOP_K * BLK, :] = sorted_rows(0, TOP_K * BLK)
    for lo in range(TOP_K * BLK, SORT_ROWS, CHUNK):
        @pl.when(lo < n_rows)
        def _(lo=lo):
            buf[slot, lo:lo + CHUNK, :] = sorted_rows(lo, CHUNK)

    def issue(j, c):
        row = pl.multiple_of(dst_ref[b * N_PIECES + j], PIECE)
        src_row = pl.multiple_of(j * PIECE, PIECE)
        pltpu.make_async_copy(buf.at[slot, pl.ds(src_row, PIECE), :], xs_hbm.at[pl.ds(row, PIECE), :],
                              sem.at[slot]).start()
        return c
    lax.fori_loop(0, npc_ref[b], issue, 0)

    @pl.when(b == n_b - 1)
    def _():
        _wait_pieces(npc_ref[b], N_PIECES, xs_hbm, sem.at[slot])

        @pl.when(n_b >= 2)
        def _():
            _wait_pieces(npc_ref[b - 1], N_PIECES, xs_hbm, sem.at[1 - slot])

        _wait_pieces(n_tail, tail_start_ref.shape[0] * TAIL_PIECES, xs_hbm, zsem)

        def tile_body(t, c):
            unused_tile_copy(t).wait()
            return c
        lax.fori_loop(used_tiles, n_tiles, tile_body, 0)


def _dispatch(u2, meta, dst, n_pieces, tail_start, tail_count, misc, n_tiles):
    n_tok, d = u2.shape
    n_blocks = n_tok // BLK
    return pl.pallas_call(
        _dispatch_kernel,
        grid_spec=pltpu.PrefetchScalarGridSpec(
            num_scalar_prefetch=5,
            grid=(n_blocks,),
            in_specs=[pl.BlockSpec((BLK, d), lambda b, *_: (b, 0)), pl.BlockSpec((BLK, LANES), lambda b, *_: (b, 0))],
            out_specs=pl.BlockSpec(memory_space=pl.ANY),
            scratch_shapes=[pltpu.VMEM((2, SORT_ROWS, d), BF16), pltpu.VMEM((TILE, d), BF16),
                            pltpu.SemaphoreType.DMA((2,)), pltpu.SemaphoreType.DMA(()), pltpu.SemaphoreType.DMA(())],
        ),
        out_shape=jax.ShapeDtypeStruct((n_tiles * TILE, d), BF16),
        compiler_params=pltpu.CompilerParams(dimension_semantics=("arbitrary",), vmem_limit_bytes=VMEM_LIMIT),
        name="dispatch",
    )(dst, n_pieces, tail_start, tail_count, misc, u2, meta)


def _deinterleave_perm():
    src = lax.broadcasted_iota(jnp.int32, (MXU_DIM, MXU_DIM), 0)
    dst = lax.broadcasted_iota(jnp.int32, (MXU_DIM, MXU_DIM), 1)
    want = jnp.where(dst < LANES, 2 * dst, 2 * (dst - LANES) + 1)
    return jnp.where(src == want, 1.0, 0.0).astype(BF16)


def _load_expert_weights(wgu_ref, wg_s, wl_s):
    perm = _deinterleave_perm()
    d_ff = wg_s.shape[1]
    for c in range(2 * d_ff // MXU_DIM):
        chunk = wgu_ref[:, c * MXU_DIM:(c + 1) * MXU_DIM].astype(BF16)
        o = _dot(chunk, perm)
        wg_s[:, c * LANES:(c + 1) * LANES] = o[:, :LANES].astype(BF16)
        wl_s[:, c * LANES:(c + 1) * LANES] = o[:, LANES:].astype(BF16)


def _expert_mlp(x, wg, wl, bg, bl, wd, bd):
    x_glu = jnp.minimum(_dot(x, wg) + bg, SWIGLU_LIMIT)
    x_lin = jnp.clip(_dot(x, wl) + bl, -SWIGLU_LIMIT, SWIGLU_LIMIT)
    act = x_glu * jax.nn.sigmoid(SWIGLU_ALPHA * x_glu) * (x_lin + 1.0)
    return _dot(act.astype(BF16), wd) + bd


def _experts_kernel(texp_ref, tblk_ref, tvalid_ref, tfirst_ref, tnext_ref, x_ref, bg_ref, bl_ref, bd_ref,
                    wgu_hbm, wd_hbm, y_ref, wgu_f32, wd_f32, wg_s, wl_s, wd_s, sem):
    del tblk_ref
    i = pl.program_id(0)

    def fetch(e):
        return (pltpu.make_async_copy(wgu_hbm.at[e], wgu_f32, sem.at[0]),
                pltpu.make_async_copy(wd_hbm.at[e], wd_f32, sem.at[1]))

    @pl.when(tfirst_ref[i] == 1)
    def _():
        @pl.when(i == 0)
        def _():
            for cp in fetch(texp_ref[0]):
                cp.start()

        for cp in fetch(texp_ref[i]):
            cp.wait()
        _load_expert_weights(wgu_f32, wg_s, wl_s)
        wd_s[...] = wd_f32[...].astype(BF16)

        @pl.when(tnext_ref[i] >= 0)
        def _():
            for cp in fetch(tnext_ref[i]):
                cp.start()

    @pl.when(tvalid_ref[i] == 1)
    def _():
        y = _expert_mlp(x_ref[...], wg_s[...], wl_s[...], bg_ref[0], bl_ref[0], wd_s[...], bd_ref[0])
        y_ref[...] = y.astype(BF16)

    @pl.when(tvalid_ref[i] == 0)
    def _():
        y_ref[...] = jnp.zeros(y_ref.shape, y_ref.dtype)


def _experts(xs, t_exp, t_blk, t_valid, t_first, t_next, w_gu, b_g, b_l, w_d, b_d):
    n_rows, d = xs.shape
    d_ff = w_gu.shape[2] // 2
    rows = pl.BlockSpec((TILE, d), lambda i, te, tb, *_: (tb[i], 0))
    per_e = lambda a: pl.BlockSpec((1,) + a.shape[1:], lambda i, te, *_: (te[i],) + (0,) * (a.ndim - 1))
    hbm = pl.BlockSpec(memory_space=pl.ANY)
    return pl.pallas_call(
        _experts_kernel,
        grid_spec=pltpu.PrefetchScalarGridSpec(
            num_scalar_prefetch=5,
            grid=(n_rows // TILE,),
            in_specs=[rows, per_e(b_g), per_e(b_l), per_e(b_d), hbm, hbm],
            out_specs=pl.BlockSpec((TILE, d), lambda i, *_: (i, 0)),
            scratch_shapes=[pltpu.VMEM(w_gu.shape[1:], F32), pltpu.VMEM(w_d.shape[1:], F32),
                            pltpu.VMEM((d, d_ff), BF16), pltpu.VMEM((d, d_ff), BF16), pltpu.VMEM((d_ff, d), BF16),
                            pltpu.SemaphoreType.DMA((2,))],
        ),
        out_shape=jax.ShapeDtypeStruct((n_rows, d), BF16),
        compiler_params=pltpu.CompilerParams(dimension_semantics=("arbitrary",), vmem_limit_bytes=VMEM_LIMIT),
        name="experts",
    )(t_exp, t_blk, t_valid, t_first, t_next, xs, b_g, b_l, b_d, w_gu, w_d)


def _combine_kernel(dst_ref, npc_ref, meta_ref, h1_ref, gfin_ref, ys_hbm, out_ref, ybuf, acc_ref, sem):
    b = pl.program_id(0)
    n_b = pl.num_programs(0)
    slot = b % 2

    def start_block(blk, slot_):
        def body(j, c):
            row = pl.multiple_of(dst_ref[blk * N_PIECES + j], PIECE)
            dst_row = pl.multiple_of(j * PIECE, PIECE)
            pltpu.make_async_copy(ys_hbm.at[pl.ds(row, PIECE), :], ybuf.at[slot_, pl.ds(dst_row, PIECE), :],
                                  sem.at[slot_]).start()
            return c
        lax.fori_loop(0, npc_ref[blk], body, 0)

    @pl.when(b == 0)
    def _():
        ybuf[...] = jnp.zeros(ybuf.shape, ybuf.dtype)
        start_block(0, 0)

    @pl.when(b + 1 < n_b)
    def _():
        start_block(b + 1, 1 - slot)

    meta = meta_ref[...]

    def gate_weights(lo, n):
        r = lax.broadcasted_iota(jnp.int32, (BLK, n), 1).astype(F32)
        w = jnp.zeros((BLK, n), F32)
        for k in range(TOP_K):
            w = jnp.where(r == meta[:, k:k + 1] - float(lo), meta[:, TOP_K + k:TOP_K + k + 1], w)
        return w.astype(BF16)

    w_base = gate_weights(0, TOP_K * BLK)
    _wait_pieces(npc_ref[b], N_PIECES, ys_hbm, sem.at[slot])
    acc_ref[...] = h1_ref[...] + _dot(w_base, ybuf[slot, 0:TOP_K * BLK, :])
    n_rows = npc_ref[b] * PIECE
    for lo in range(TOP_K * BLK, SORT_ROWS, CHUNK):
        @pl.when(lo < n_rows)
        def _(lo=lo):
            acc_ref[...] += _dot(gate_weights(lo, CHUNK), ybuf[slot, lo:lo + CHUNK, :])

    out_ref[...] = _rms(acc_ref[...], gfin_ref[...])


def _combine(ys, meta, h1, g_final, dst, n_pieces):
    n_tok, d = h1.shape
    return pl.pallas_call(
        _combine_kernel,
        grid_spec=pltpu.PrefetchScalarGridSpec(
            num_scalar_prefetch=2,
            grid=(n_tok // BLK,),
            in_specs=[pl.BlockSpec((BLK, LANES), lambda b, *_: (b, 0)), pl.BlockSpec((BLK, d), lambda b, *_: (b, 0)),
                      pl.BlockSpec(g_final.shape, lambda b, *_: (0, 0)), pl.BlockSpec(memory_space=pl.ANY)],
            out_specs=pl.BlockSpec((BLK, d), lambda b, *_: (b, 0)),
            scratch_shapes=[pltpu.VMEM((2, SORT_ROWS, d), BF16), pltpu.VMEM((BLK, d), F32),
                            pltpu.SemaphoreType.DMA((2,))],
        ),
        out_shape=jax.ShapeDtypeStruct((n_tok, d), F32),
        compiler_params=pltpu.CompilerParams(dimension_semantics=("arbitrary",), vmem_limit_bytes=VMEM_LIMIT),
        name="combine",
    )(dst, n_pieces, meta, h1, g_final, ys)


def _block_diag(w):
    heads, n, _ = w.shape
    eye = jnp.eye(heads, dtype=w.dtype)
    return (eye[:, None, :, None] * w[:, :, None, :]).reshape(heads * n, heads * n)


def kernel(x, meta_tokens, g_mix, w_in, conv_dw_w, conv_dw_b, conv_ln_g, conv_ln_b, lru_conv_w, lru_conv_b,
           lru_wa, lru_ba, lru_wx, lru_bx, lru_lambda, w_out, g_ffn, w_router, b_router, w_gate_up, b_gate_up,
           w_down, b_down, g_final):
    bsz, seq, d = x.shape
    n_e = w_gate_up.shape[1]
    assert w_in.shape[0] == 1 and seq % BLK == 0 and n_e == N_EXPERTS
    n_blocks = bsz * seq // BLK
    row = lambda v: v.reshape(1, -1)
    params = {
        "g_mix": row(g_mix[0]), "w_in": w_in[0].astype(BF16),
        "dw_w": conv_dw_w[0], "dw_b": row(conv_dw_b[0]), "ln_g": row(conv_ln_g[0]), "ln_b": row(conv_ln_b[0]),
        "lc_w": lru_conv_w[0], "lc_b": row(lru_conv_b[0]),
        "w_ax": jnp.concatenate([_block_diag(lru_wa[0]), _block_diag(lru_wx[0])], axis=1).astype(BF16),
        "b_ax": row(jnp.concatenate([lru_ba[0], lru_bx[0]])), "lam": row(lru_lambda[0]),
        "w_out": w_out[0].astype(BF16), "g_ffn": row(g_ffn[0]),
        "w_r": w_router[0].astype(BF16), "b_r": row(b_router[0]),
    }
    h1, u2, meta, cnt = _mixer(x, meta_tokens, params)

    n_tiles = -(-(n_blocks * (TOP_K * BLK + n_e * (PIECE - 1))) // TILE) + n_e
    (dst, n_pieces, tail_start, tail_count, misc, t_exp, t_blk, t_valid, t_first,
     t_next) = _routing_tables(cnt[:, 0, :].astype(jnp.int32), n_tiles)
    xs = _dispatch(u2, meta, dst, n_pieces, tail_start, tail_count, misc, n_tiles)
    b_gu = b_gate_up[0].reshape(n_e, 1, -1, 2)
    ys = _experts(xs, t_exp, t_blk, t_valid, t_first, t_next, w_gate_up.reshape(w_gate_up.shape[1:]),
                  b_gu[..., 0], b_gu[..., 1], w_down.reshape(w_down.shape[1:]), b_down[0].reshape(n_e, 1, -1))
    out = _combine(ys, meta, h1, row(g_final), dst, n_pieces)
    return out.reshape(bsz, seq, d)
```

```python
import jax
import jax.numpy as jnp
from jax import lax
from jax.experimental import pallas as pl
from jax.experimental.pallas import tpu as pltpu

D_MODEL = 1024
N_META = 16
CONV_W = 512
LRU_W = 512
LRU_HEADS = 8
CONV_K = 31
LRU_CONV_K = 4
RG_C = 8.0
N_EXPERTS = 32
TOP_K = 4
SWIGLU_ALPHA = 1.702
SWIGLU_LIMIT = 7.0
NORM_EPS = 1e-6
LN_EPS = 1e-5

SUBLANES = 8
LANES = 128
MXU_DIM = 256
VMEM_LIMIT = 58 * 1024 * 1024

BLK = 512
CONV_HALO = 32
LRU_HALO = 8
PIECE = 16
CHUNK = 256
SORT_ROWS = -(-(TOP_K * BLK + N_EXPERTS * (PIECE - 1)) // CHUNK) * CHUNK
N_PIECES = SORT_ROWS // PIECE
BASE_ROWS = TOP_K * BLK
TILE = 1024
SUB = 256
TAIL_PIECES = TILE // PIECE - 1

BF16 = jnp.bfloat16
F32 = jnp.float32


def _dot(a, b):
    return jnp.dot(a, b, preferred_element_type=F32)


def _sigmoid(x):
    return 0.5 * jnp.tanh(0.5 * x) + 0.5


def _rms(x, g):
    return x * lax.rsqrt(jnp.mean(x * x, axis=-1, keepdims=True) + NORM_EPS) * g


def _mix_rows(h, is_seq_start, p, s):
    rows = h.shape[0]
    u = _rms(h, p["g_mix"][...])
    z = _dot(u.astype(BF16), p["w_in"][...])
    c_val, c_gate = z[:, :CONV_W], z[:, CONV_W:2 * CONV_W]
    r_x, r_gate = z[:, 2 * CONV_W:2 * CONV_W + LRU_W], z[:, 2 * CONV_W + LRU_W:]

    gbuf = s["gbuf"]
    gbuf[CONV_HALO:CONV_HALO + rows, :] = c_val * _sigmoid(c_gate)
    acc = jnp.broadcast_to(p["dw_b"][...], (rows, CONV_W))
    first = CONV_HALO - (CONV_K - 1)
    for phase in range(SUBLANES):
        taps = [k for k in range(CONV_K) if (first + k) % SUBLANES == phase]
        span = max((first + k) // SUBLANES for k in taps) * SUBLANES + rows
        src = gbuf
        if phase:
            src = s["sbuf"]
            src[0:span, :] = gbuf[phase:phase + span, :]
        for k in taps:
            off = (first + k) // SUBLANES * SUBLANES
            acc = acc + p["dw_w"][k:k + 1, :] * src[off:off + rows, :]
    gbuf[0:CONV_HALO, :] = gbuf[rows:rows + CONV_HALO, :]
    mu = jnp.mean(acc, axis=-1, keepdims=True)
    xc = acc - mu
    y = xc * lax.rsqrt(jnp.mean(xc * xc, axis=-1, keepdims=True) + LN_EPS)
    y = y * p["ln_g"][...] + p["ln_b"][...]
    conv_out = y * _sigmoid(y)

    rbuf = s["rbuf"]
    rbuf[LRU_HALO:LRU_HALO + rows, :] = r_x
    xr = jnp.broadcast_to(p["lc_b"][...], (rows, LRU_W))
    for k in range(LRU_CONV_K):
        xr = xr + p["lc_w"][k:k + 1, :] * rbuf[LRU_HALO - (LRU_CONV_K - 1) + k:LRU_HALO - (LRU_CONV_K - 1) + k + rows, :]
    rbuf[0:LRU_HALO, :] = rbuf[rows:rows + LRU_HALO, :]
    ax = _dot(xr.astype(BF16), p["w_ax"][...]) + p["b_ax"][...]
    r = _sigmoid(ax[:, :LRU_W])
    i = _sigmoid(ax[:, LRU_W:])
    log_a = (-RG_C) * r * jax.nn.softplus(-p["lam"][...])
    a = jnp.exp(log_a)
    t = jnp.tanh(log_a)
    mult = jnp.sqrt(-2.0 * t / (1.0 - t))
    if is_seq_start:
        row = lax.broadcasted_iota(jnp.int32, (rows, LRU_W), 0)
        mult = jnp.where(row == 0, 1.0, mult)
    b = mult * (i * xr)

    a = a.reshape(rows // SUBLANES, SUBLANES, LRU_W)
    b = b.reshape(rows // SUBLANES, SUBLANES, LRU_W)
    sub = lax.broadcasted_iota(jnp.int32, a.shape, 1)
    for d in (1, 2, 4):
        keep = sub >= d
        a_sh = pltpu.roll(a, d, 1)
        b_sh = pltpu.roll(b, d, 1)
        b = jnp.where(keep, a * b_sh + b, b)
        a = jnp.where(keep, a * a_sh, a)
    s["abuf"][0:rows, :] = a.reshape(rows, LRU_W)
    s["bbuf"][0:rows, :] = b.reshape(rows, LRU_W)

    def carry_step(j, hprev):
        off = pl.multiple_of(j * SUBLANES, SUBLANES)
        hj = s["abuf"][pl.ds(off, SUBLANES), :] * hprev + s["bbuf"][pl.ds(off, SUBLANES), :]
        s["bbuf"][pl.ds(off, SUBLANES), :] = hj
        return hj[SUBLANES - 1:SUBLANES, :]

    s["hcar"][...] = lax.fori_loop(0, rows // SUBLANES, carry_step, s["hcar"][...])
    lru_out = s["bbuf"][0:rows, :] * jax.nn.gelu(r_gate, approximate=True)
    return conv_out, lru_out


def _route(logits):
    rows, n_e = logits.shape
    lane = lax.broadcasted_iota(jnp.int32, (rows, n_e), 1).astype(F32)
    masked = logits
    vals, sels = [], []
    for _ in range(TOP_K):
        m = jnp.max(masked, axis=-1, keepdims=True)
        first = jnp.min(jnp.where(masked == m, lane, float(n_e)), axis=-1, keepdims=True)
        sel = lane == first
        vals.append(m)
        sels.append(sel)
        masked = jnp.where(sel, -jnp.inf, masked)
    exps = [jnp.exp(v - vals[0]) for v in vals]
    denom = exps[0] + exps[1] + exps[2] + exps[3]

    chosen = jnp.zeros_like(logits)
    for sel in sels:
        chosen = jnp.where(sel, 1.0, chosen)
    earlier = (lax.broadcasted_iota(jnp.int32, (rows, rows), 1) < lax.broadcasted_iota(jnp.int32, (rows, rows), 0))
    rank = _dot(jnp.where(earlier, 1.0, 0.0).astype(BF16), chosen.astype(BF16))
    count = jnp.sum(chosen, axis=0, keepdims=True)
    pieces = jnp.floor((count + (PIECE - 1.0)) * (1.0 / PIECE))
    below = (lax.broadcasted_iota(jnp.int32, (n_e, n_e), 0) < lax.broadcasted_iota(jnp.int32, (n_e, n_e), 1))
    seg_start = _dot(jnp.broadcast_to(pieces, (SUBLANES, n_e)).astype(BF16),
                     jnp.where(below, 1.0, 0.0).astype(BF16))[0:1, :] * float(PIECE)
    pos = seg_start + rank

    mlane = lax.broadcasted_iota(jnp.int32, (rows, LANES), 1)
    meta = jnp.zeros((rows, LANES), F32)
    for k in range(TOP_K):
        pos_k = jnp.sum(jnp.where(sels[k], pos, 0.0), axis=-1, keepdims=True)
        meta = jnp.where(mlane == k, pos_k, meta)
        meta = jnp.where(mlane == TOP_K + k, exps[k] / denom, meta)
    return meta, pieces


_MIX_PARAMS = ("g_mix", "w_in", "dw_w", "dw_b", "ln_g", "ln_b", "lc_w", "lc_b", "w_ax", "b_ax", "lam",
               "w_out", "g_ffn", "w_r", "b_r")
_MIX_SCRATCH = ("gbuf", "sbuf", "rbuf", "abuf", "bbuf", "hcar")


def _mixer_kernel(x_ref, meta_tok_ref, *refs):
    n_p = len(_MIX_PARAMS)
    p = dict(zip(_MIX_PARAMS, refs[:n_p]))
    h1_ref, u2_ref, meta_ref, cnt_ref = refs[n_p:n_p + 4]
    s = dict(zip(_MIX_SCRATCH, refs[n_p + 4:]))

    @pl.when(pl.program_id(1) == 0)
    def _():
        s["gbuf"][0:CONV_HALO, :] = jnp.zeros((CONV_HALO, CONV_W), F32)
        s["rbuf"][0:LRU_HALO, :] = jnp.zeros((LRU_HALO, LRU_W), F32)
        s["hcar"][...] = jnp.zeros((1, LRU_W), F32)
        _mix_rows(meta_tok_ref[...], True, p, s)

    h = x_ref[0]
    conv_out, lru_out = _mix_rows(h, False, p, s)
    w_out = p["w_out"]
    h1 = h + _dot(conv_out.astype(BF16), w_out[0:CONV_W, :]) + _dot(lru_out.astype(BF16), w_out[CONV_W:, :])
    h1_ref[...] = h1
    u2 = _rms(h1, p["g_ffn"][...]).astype(BF16)
    u2_ref[...] = u2
    logits = _dot(u2, p["w_r"][...]) + p["b_r"][...]
    meta, pieces = _route(logits)
    meta_ref[...] = meta
    cnt_ref[0] = pieces


def _mixer(x, meta_tok, params):
    bsz, seq, d = x.shape
    n_chunks = seq // BLK
    n_tok = bsz * seq

    def const(a):
        return pl.BlockSpec(a.shape, lambda b, c: (0,) * a.ndim)

    row_block = lambda width: pl.BlockSpec((BLK, width), lambda b, c: (b * n_chunks + c, 0))
    return pl.pallas_call(
        _mixer_kernel,
        grid=(bsz, n_chunks),
        in_specs=[pl.BlockSpec((1, BLK, d), lambda b, c: (b, c, 0)), const(meta_tok)]
        + [const(params[k]) for k in _MIX_PARAMS],
        out_specs=[row_block(d), row_block(d), row_block(LANES),
                   pl.BlockSpec((1, 1, N_EXPERTS), lambda b, c: (b * n_chunks + c, 0, 0))],
        out_shape=[jax.ShapeDtypeStruct((n_tok, d), F32), jax.ShapeDtypeStruct((n_tok, d), BF16),
                   jax.ShapeDtypeStruct((n_tok, LANES), F32),
                   jax.ShapeDtypeStruct((n_tok // BLK, 1, N_EXPERTS), F32)],
        scratch_shapes=[pltpu.VMEM((CONV_HALO + BLK, CONV_W), F32), pltpu.VMEM((CONV_HALO + BLK, CONV_W), F32),
                        pltpu.VMEM((LRU_HALO + BLK, LRU_W), F32),
                        pltpu.VMEM((BLK, LRU_W), F32), pltpu.VMEM((BLK, LRU_W), F32),
                        pltpu.VMEM((1, LRU_W), F32)],
        compiler_params=pltpu.CompilerParams(dimension_semantics=("arbitrary", "arbitrary"),
                                             vmem_limit_bytes=VMEM_LIMIT),
        name="mixer",
    )(x, meta_tok, *[params[k] for k in _MIX_PARAMS])


def _routing_tables(cnt, n_tiles):
    n_blocks, n_e = cnt.shape
    tile_pieces = TILE // PIECE
    per_expert = cnt.sum(0)
    tiles_e = (per_expert + tile_pieces - 1) // tile_pieces
    tile_end = jnp.cumsum(tiles_e)
    tile_start = tile_end - tiles_e
    seg_global = tile_start[None, :] * tile_pieces + jnp.cumsum(cnt, axis=0) - cnt
    seg_end_local = jnp.cumsum(cnt, axis=1)
    seg_local = seg_end_local - cnt
    n_pieces = seg_end_local[:, -1]

    j = jnp.arange(N_PIECES, dtype=jnp.int32)
    shift = seg_global - seg_local
    step = jnp.concatenate([shift[:, :1], shift[:, 1:] - shift[:, :-1]], axis=1)
    dst = j[None, :] + jnp.where(j[None, :, None] >= seg_local[:, None, :], step[:, None, :], 0).sum(-1)
    dst = jnp.where(j[None, :] < n_pieces[:, None], dst * PIECE, 0)

    tail_start = (tile_start * tile_pieces + per_expert) * PIECE
    tail_count = tiles_e * tile_pieces - per_expert

    total = tile_end[-1]
    i = jnp.arange(n_tiles, dtype=jnp.int32)
    valid = i < total
    idx = jnp.where(valid, i, total - 1)
    t_exp = jnp.minimum((idx[:, None] >= tile_end[None, :]).sum(-1), n_e - 1)
    t_first = valid & ((i[:, None] == tile_start[None, :]) & (tiles_e[None, :] > 0)).any(-1)
    left = per_expert[None, :] * PIECE - (i[:, None] - tile_start[None, :]) * TILE
    rows_i = jnp.where(t_exp[:, None] == jnp.arange(n_e)[None, :], jnp.clip(left, 0, TILE), 0).sum(-1)
    t_sub = jnp.where(valid, (rows_i + SUB - 1) // SUB, 0)
    e_ids = jnp.arange(n_e, dtype=jnp.int32)
    later = (e_ids[None, :] > t_exp[:, None]) & (tiles_e[None, :] > 0)
    t_next = jnp.where(later, e_ids[None, :], n_e).min(-1)
    t_next = jnp.where(t_next < n_e, t_next, -1)
    i32 = lambda a: a.astype(jnp.int32)
    misc = jnp.stack([total, tail_count.sum()])
    return (i32(dst).reshape(-1), i32(n_pieces), i32(tail_start), i32(tail_count), i32(misc),
            i32(t_exp), i32(idx), i32(t_sub), i32(t_first), i32(t_next))


def _wait_pieces(count, max_count, hbm_ref, sem):
    bit = 1
    while bit <= max_count:
        @pl.when((count & bit) != 0)
        def _(rows=bit * PIECE):
            pltpu.make_async_copy(hbm_ref.at[0:rows, :], hbm_ref.at[0:rows, :], sem).wait()
        bit *= 2


def _dispatch_kernel(dst_ref, npc_ref, tail_start_ref, tail_count_ref, misc_ref, u2_ref, meta_ref, xs_hbm,
                     buf, zbuf, sem, zsem, tsem):
    b = pl.program_id(0)
    n_b = pl.num_programs(0)
    slot = b % 2
    n_tiles = xs_hbm.shape[0] // TILE
    used_tiles, n_tail = misc_ref[0], misc_ref[1]

    @pl.when(b >= 2)
    def _():
        _wait_pieces(npc_ref[b - 2], N_PIECES, xs_hbm, sem.at[slot])

    def unused_tile_copy(t):
        return pltpu.make_async_copy(zbuf, xs_hbm.at[pl.ds(pl.multiple_of(t * TILE, TILE), TILE), :], tsem)

    @pl.when(b == 0)
    def _():
        zbuf[...] = jnp.zeros(zbuf.shape, zbuf.dtype)

        def tile_body(t, c):
            unused_tile_copy(t).start()
            return c
        lax.fori_loop(used_tiles, n_tiles, tile_body, 0)

        def expert_body(e, c):
            count = tail_count_ref[e]
            row = tail_start_ref[e]
            bit = 1 << (TAIL_PIECES.bit_length() - 1)
            while bit:
                @pl.when((count & bit) != 0)
                def _(row=row, n=bit * PIECE):
                    pltpu.make_async_copy(zbuf.at[0:n, :], xs_hbm.at[pl.ds(pl.multiple_of(row, PIECE), n), :],
                                          zsem).start()
                row = row + (count & bit) * PIECE
                bit //= 2
            return c
        lax.fori_loop(0, tail_start_ref.shape[0], expert_body, 0)

    pos_t = meta_ref[...].T
    u = u2_ref[...]

    def sorted_rows(lo, n):
        r = (lax.broadcasted_iota(jnp.int32, (n, BLK), 0) + lo).astype(F32)
        onehot = jnp.zeros((n, BLK), F32)
        for k in range(TOP_K):
            onehot = jnp.where(r == pos_t[k:k + 1, :], 1.0, onehot)
        return _dot(onehot.astype(BF16), u).astype(BF16)

    n_rows = npc_ref[b] * PIECE
    buf[slot, 0:BASE_ROWS, :] = sorted_rows(0, BASE_ROWS)
    for lo in range(BASE_ROWS, SORT_ROWS, CHUNK):
        @pl.when(lo < n_rows)
        def _(lo=lo):
            buf[slot, lo:lo + CHUNK, :] = sorted_rows(lo, CHUNK)

    def issue(j, c):
        row = pl.multiple_of(dst_ref[b * N_PIECES + j], PIECE)
        src_row = pl.multiple_of(j * PIECE, PIECE)
        pltpu.make_async_copy(buf.at[slot, pl.ds(src_row, PIECE), :], xs_hbm.at[pl.ds(row, PIECE), :],
                              sem.at[slot]).start()
        return c
    lax.fori_loop(0, npc_ref[b], issue, 0)

    @pl.when(b == n_b - 1)
    def _():
        _wait_pieces(npc_ref[b], N_PIECES, xs_hbm, sem.at[slot])

        @pl.when(n_b >= 2)
        def _():
            _wait_pieces(npc_ref[b - 1], N_PIECES, xs_hbm, sem.at[1 - slot])

        _wait_pieces(n_tail, tail_start_ref.shape[0] * TAIL_PIECES, xs_hbm, zsem)

        def tile_body(t, c):
            unused_tile_copy(t).wait()
            return c
        lax.fori_loop(used_tiles, n_tiles, tile_body, 0)


def _dispatch(u2, meta, dst, n_pieces, tail_start, tail_count, misc, n_tiles):
    n_tok, d = u2.shape
    n_blocks = n_tok // BLK
    return pl.pallas_call(
        _dispatch_kernel,
        grid_spec=pltpu.PrefetchScalarGridSpec(
            num_scalar_prefetch=5,
            grid=(n_blocks,),
            in_specs=[pl.BlockSpec((BLK, d), lambda b, *_: (b, 0)), pl.BlockSpec((BLK, LANES), lambda b, *_: (b, 0))],
            out_specs=pl.BlockSpec(memory_space=pl.ANY),
            scratch_shapes=[pltpu.VMEM((2, SORT_ROWS, d), BF16), pltpu.VMEM((TILE, d), BF16),
                            pltpu.SemaphoreType.DMA((2,)), pltpu.SemaphoreType.DMA(()), pltpu.SemaphoreType.DMA(())],
        ),
        out_shape=jax.ShapeDtypeStruct((n_tiles * TILE, d), BF16),
        compiler_params=pltpu.CompilerParams(dimension_semantics=("arbitrary",), vmem_limit_bytes=VMEM_LIMIT),
        name="dispatch",
    )(dst, n_pieces, tail_start, tail_count, misc, u2, meta)


def _deinterleave_perm():
    src = lax.broadcasted_iota(jnp.int32, (MXU_DIM, MXU_DIM), 0)
    dst = lax.broadcasted_iota(jnp.int32, (MXU_DIM, MXU_DIM), 1)
    want = jnp.where(dst < LANES, 2 * dst, 2 * (dst - LANES) + 1)
    return jnp.where(src == want, 1.0, 0.0).astype(BF16)


def _load_expert_weights(wgu_ref, wg_s, wl_s):
    perm = _deinterleave_perm()
    d_ff = wg_s.shape[1]
    for c in range(2 * d_ff // MXU_DIM):
        chunk = wgu_ref[:, c * MXU_DIM:(c + 1) * MXU_DIM].astype(BF16)
        o = _dot(chunk, perm)
        wg_s[:, c * LANES:(c + 1) * LANES] = o[:, :LANES].astype(BF16)
        wl_s[:, c * LANES:(c + 1) * LANES] = o[:, LANES:].astype(BF16)


def _expert_mlp(x, wg, wl, bg, bl, wd, bd):
    x_glu = jnp.minimum(_dot(x, wg) + bg, SWIGLU_LIMIT)
    x_lin = jnp.clip(_dot(x, wl) + bl, -SWIGLU_LIMIT, SWIGLU_LIMIT)
    act = x_glu * _sigmoid(SWIGLU_ALPHA * x_glu) * (x_lin + 1.0)
    return _dot(act.astype(BF16), wd) + bd


def _experts_kernel(texp_ref, tblk_ref, tsub_ref, tfirst_ref, tnext_ref, x_ref, bg_ref, bl_ref, bd_ref,
                    wgu_hbm, wd_hbm, y_ref, wgu_f32, wd_f32, wg_s, wl_s, wd_s, sem):
    del tblk_ref
    i = pl.program_id(0)

    def fetch(e):
        return (pltpu.make_async_copy(wgu_hbm.at[e], wgu_f32, sem.at[0]),
                pltpu.make_async_copy(wd_hbm.at[e], wd_f32, sem.at[1]))

    @pl.when(tfirst_ref[i] == 1)
    def _():
        @pl.when(i == 0)
        def _():
            for cp in fetch(texp_ref[0]):
                cp.start()

        for cp in fetch(texp_ref[i]):
            cp.wait()
        _load_expert_weights(wgu_f32, wg_s, wl_s)
        wd_s[...] = wd_f32[...].astype(BF16)

        @pl.when(tnext_ref[i] >= 0)
        def _():
            for cp in fetch(tnext_ref[i]):
                cp.start()

    def mlp_pass(q, c):
        off = pl.multiple_of(q * SUB, SUB)
        y = _expert_mlp(x_ref[pl.ds(off, SUB), :], wg_s[...], wl_s[...], bg_ref[0], bl_ref[0], wd_s[...], bd_ref[0])
        y_ref[pl.ds(off, SUB), :] = y.astype(BF16)
        return c
    lax.fori_loop(0, tsub_ref[i], mlp_pass, 0)

    def zero_pass(q, c):
        y_ref[pl.ds(pl.multiple_of(q * SUB, SUB), SUB), :] = jnp.zeros((SUB, y_ref.shape[1]), y_ref.dtype)
        return c
    lax.fori_loop(tsub_ref[i], TILE // SUB, zero_pass, 0)


def _experts(xs, t_exp, t_blk, t_sub, t_first, t_next, w_gu, b_g, b_l, w_d, b_d):
    n_rows, d = xs.shape
    d_ff = w_gu.shape[2] // 2
    rows = pl.BlockSpec((TILE, d), lambda i, te, tb, *_: (tb[i], 0))
    per_e = lambda a: pl.BlockSpec((1,) + a.shape[1:], lambda i, te, *_: (te[i],) + (0,) * (a.ndim - 1))
    hbm = pl.BlockSpec(memory_space=pl.ANY)
    return pl.pallas_call(
        _experts_kernel,
        grid_spec=pltpu.PrefetchScalarGridSpec(
            num_scalar_prefetch=5,
            grid=(n_rows // TILE,),
            in_specs=[rows, per_e(b_g), per_e(b_l), per_e(b_d), hbm, hbm],
            out_specs=pl.BlockSpec((TILE, d), lambda i, *_: (i, 0)),
            scratch_shapes=[pltpu.VMEM(w_gu.shape[1:], F32), pltpu.VMEM(w_d.shape[1:], F32),
                            pltpu.VMEM((d, d_ff), BF16), pltpu.VMEM((d, d_ff), BF16), pltpu.VMEM((d_ff, d), BF16),
                            pltpu.SemaphoreType.DMA((2,))],
        ),
        out_shape=jax.ShapeDtypeStruct((n_rows, d), BF16),
        compiler_params=pltpu.CompilerParams(dimension_semantics=("arbitrary",), vmem_limit_bytes=VMEM_LIMIT),
        name="experts",
    )(t_exp, t_blk, t_sub, t_first, t_next, xs, b_g, b_l, b_d, w_gu, w_d)


def _combine_kernel(dst_ref, npc_ref, meta_ref, h1_ref, gfin_ref, ys_hbm, out_ref, ybuf, acc_ref, sem):
    b = pl.program_id(0)
    n_b = pl.num_programs(0)
    slot = b % 2

    def start_block(blk, slot_):
        def body(j, c):
            row = pl.multiple_of(dst_ref[blk * N_PIECES + j], PIECE)
            dst_row = pl.multiple_of(j * PIECE, PIECE)
            pltpu.make_async_copy(ys_hbm.at[pl.ds(row, PIECE), :], ybuf.at[slot_, pl.ds(dst_row, PIECE), :],
                                  sem.at[slot_]).start()
            return c
        lax.fori_loop(0, npc_ref[blk], body, 0)

    @pl.when(b == 0)
    def _():
        ybuf[...] = jnp.zeros(ybuf.shape, ybuf.dtype)
        start_block(0, 0)

    @pl.when(b + 1 < n_b)
    def _():
        start_block(b + 1, 1 - slot)

    meta = meta_ref[...]

    def gate_weights(lo, n):
        r = lax.broadcasted_iota(jnp.int32, (BLK, n), 1).astype(F32)
        w = jnp.zeros((BLK, n), F32)
        for k in range(TOP_K):
            w = jnp.where(r == meta[:, k:k + 1] - float(lo), meta[:, TOP_K + k:TOP_K + k + 1], w)
        return w.astype(BF16)

    w_base = gate_weights(0, BASE_ROWS)
    _wait_pieces(npc_ref[b], N_PIECES, ys_hbm, sem.at[slot])
    acc_ref[...] = h1_ref[...] + _dot(w_base, ybuf[slot, 0:BASE_ROWS, :])
    n_rows = npc_ref[b] * PIECE
    for lo in range(BASE_ROWS, SORT_ROWS, CHUNK):
        @pl.when(lo < n_rows)
        def _(lo=lo):
            acc_ref[...] += _dot(gate_weights(lo, CHUNK), ybuf[slot, lo:lo + CHUNK, :])

    out_ref[...] = _rms(acc_ref[...], gfin_ref[...])


def _combine(ys, meta, h1, g_final, dst, n_pieces):
    n_tok, d = h1.shape
    n_blocks = n_tok // BLK
    return pl.pallas_call(
        _combine_kernel,
        grid_spec=pltpu.PrefetchScalarGridSpec(
            num_scalar_prefetch=2,
            grid=(n_blocks,),
            in_specs=[pl.BlockSpec((BLK, LANES), lambda b, *_: (b, 0)), pl.BlockSpec((BLK, d), lambda b, *_: (b, 0)),
                      pl.BlockSpec(g_final.shape, lambda b, *_: (0, 0)), pl.BlockSpec(memory_space=pl.ANY)],
            out_specs=pl.BlockSpec((BLK, d), lambda b, *_: (b, 0)),
            scratch_shapes=[pltpu.VMEM((2, SORT_ROWS, d), BF16), pltpu.VMEM((BLK, d), F32),
                            pltpu.SemaphoreType.DMA((2,))],
        ),
        out_shape=jax.ShapeDtypeStruct((n_tok, d), F32),
        compiler_params=pltpu.CompilerParams(dimension_semantics=("arbitrary",), vmem_limit_bytes=VMEM_LIMIT),
        name="combine",
    )(dst, n_pieces, meta, h1, g_final, ys)


def _block_diag(w):
    heads, n, _ = w.shape
    eye = jnp.eye(heads, dtype=w.dtype)
    return (eye[:, None, :, None] * w[:, :, None, :]).reshape(heads * n, heads * n)


def kernel(x, meta_tokens, g_mix, w_in, conv_dw_w, conv_dw_b, conv_ln_g, conv_ln_b, lru_conv_w, lru_conv_b,
           lru_wa, lru_ba, lru_wx, lru_bx, lru_lambda, w_out, g_ffn, w_router, b_router, w_gate_up, b_gate_up,
           w_down, b_down, g_final):
    bsz, seq, d = x.shape
    n_e = w_gate_up.shape[1]
    assert w_in.shape[0] == 1 and seq % BLK == 0 and n_e == N_EXPERTS
    n_blocks = bsz * seq // BLK
    row = lambda v: v.reshape(1, -1)
    params = {
        "g_mix": row(g_mix[0]), "w_in": w_in[0].astype(BF16),
        "dw_w": conv_dw_w[0], "dw_b": row(conv_dw_b[0]), "ln_g": row(conv_ln_g[0]), "ln_b": row(conv_ln_b[0]),
        "lc_w": lru_conv_w[0], "lc_b": row(lru_conv_b[0]),
        "w_ax": jnp.concatenate([_block_diag(lru_wa[0]), _block_diag(lru_wx[0])], axis=1).astype(BF16),
        "b_ax": row(jnp.concatenate([lru_ba[0], lru_bx[0]])), "lam": row(lru_lambda[0]),
        "w_out": w_out[0].astype(BF16), "g_ffn": row(g_ffn[0]),
        "w_r": w_router[0].astype(BF16), "b_r": row(b_router[0]),
    }
    h1, u2, meta, cnt = _mixer(x, meta_tokens, params)

    n_tiles = -(-(n_blocks * (TOP_K * BLK + n_e * (PIECE - 1))) // TILE) + n_e
    (dst, n_pieces, tail_start, tail_count, misc, t_exp, t_blk, t_sub, t_first,
     t_next) = _routing_tables(cnt[:, 0, :].astype(jnp.int32), n_tiles)
    xs = _dispatch(u2, meta, dst, n_pieces, tail_start, tail_count, misc, n_tiles)
    b_gu = b_gate_up[0].reshape(n_e, 1, -1, 2)
    ys = _experts(xs, t_exp, t_blk, t_sub, t_first, t_next, w_gate_up.reshape(w_gate_up.shape[1:]),
                  b_gu[..., 0], b_gu[..., 1], w_down.reshape(w_down.shape[1:]), b_down[0].reshape(n_e, 1, -1))
    out = _combine(ys, meta, h1, row(g_final), dst, n_pieces)
    return out.reshape(bsz, seq, d)
```

```python
import jax
import jax.numpy as jnp
from jax import lax
from jax.experimental import pallas as pl
from jax.experimental.pallas import tpu as pltpu

D_MODEL = 1024
N_META = 16
CONV_W = 512
LRU_W = 512
LRU_HEADS = 8
CONV_K = 31
LRU_CONV_K = 4
RG_C = 8.0
N_EXPERTS = 32
TOP_K = 4
SWIGLU_ALPHA = 1.702
SWIGLU_LIMIT = 7.0
NORM_EPS = 1e-6
LN_EPS = 1e-5

SUBLANES = 8
LANES = 128
MXU_DIM = 256
VMEM_LIMIT = 58 * 1024 * 1024

BLK = 512
CONV_HALO = 32
LRU_HALO = 8
PIECE = 16
CHUNK = 256
SORT_ROWS = -(-(TOP_K * BLK + N_EXPERTS * (PIECE - 1)) // CHUNK) * CHUNK
N_PIECES = SORT_ROWS // PIECE
BASE_ROWS = TOP_K * BLK
TILE = 1024
SUB = 256
TAIL_PIECES = TILE // PIECE - 1

BF16 = jnp.bfloat16
F32 = jnp.float32


def _dot(a, b):
    return jnp.dot(a, b, preferred_element_type=F32)


def _sigmoid(x):
    return 0.5 * jnp.tanh(0.5 * x) + 0.5


def _rms(x, g):
    return x * lax.rsqrt(jnp.mean(x * x, axis=-1, keepdims=True) + NORM_EPS) * g


def _mix_rows(h, is_seq_start, p, s):
    rows = h.shape[0]
    u = _rms(h, p["g_mix"][...])
    z = _dot(u.astype(BF16), p["w_in"][...])
    c_val, c_gate = z[:, :CONV_W], z[:, CONV_W:2 * CONV_W]
    r_x, r_gate = z[:, 2 * CONV_W:2 * CONV_W + LRU_W], z[:, 2 * CONV_W + LRU_W:]

    gbuf = s["gbuf"]
    gbuf[CONV_HALO:CONV_HALO + rows, :] = c_val * _sigmoid(c_gate)
    acc = jnp.broadcast_to(p["dw_b"][...], (rows, CONV_W))
    first = CONV_HALO - (CONV_K - 1)
    for phase in range(SUBLANES):
        taps = [k for k in range(CONV_K) if (first + k) % SUBLANES == phase]
        span = max((first + k) // SUBLANES for k in taps) * SUBLANES + rows
        src = gbuf
        if phase:
            src = s["sbuf"]
            src[0:span, :] = gbuf[phase:phase + span, :]
        for k in taps:
            off = (first + k) // SUBLANES * SUBLANES
            acc = acc + p["dw_w"][k:k + 1, :] * src[off:off + rows, :]
    gbuf[0:CONV_HALO, :] = gbuf[rows:rows + CONV_HALO, :]
    mu = jnp.mean(acc, axis=-1, keepdims=True)
    xc = acc - mu
    y = xc * lax.rsqrt(jnp.mean(xc * xc, axis=-1, keepdims=True) + LN_EPS)
    y = y * p["ln_g"][...] + p["ln_b"][...]
    conv_out = y * _sigmoid(y)

    rbuf = s["rbuf"]
    rbuf[LRU_HALO:LRU_HALO + rows, :] = r_x
    xr = jnp.broadcast_to(p["lc_b"][...], (rows, LRU_W))
    for k in range(LRU_CONV_K):
        xr = xr + p["lc_w"][k:k + 1, :] * rbuf[LRU_HALO - (LRU_CONV_K - 1) + k:LRU_HALO - (LRU_CONV_K - 1) + k + rows, :]
    rbuf[0:LRU_HALO, :] = rbuf[rows:rows + LRU_HALO, :]
    ax = _dot(xr.astype(BF16), p["w_ax"][...]) + p["b_ax"][...]
    r = _sigmoid(ax[:, :LRU_W])
    i = _sigmoid(ax[:, LRU_W:])
    log_a = (-RG_C) * r * jax.nn.softplus(-p["lam"][...])
    a = jnp.exp(log_a)
    t = jnp.tanh(log_a)
    mult = jnp.sqrt(-2.0 * t / (1.0 - t))
    if is_seq_start:
        row = lax.broadcasted_iota(jnp.int32, (rows, LRU_W), 0)
        mult = jnp.where(row == 0, 1.0, mult)
    b = mult * (i * xr)

    a = a.reshape(rows // SUBLANES, SUBLANES, LRU_W)
    b = b.reshape(rows // SUBLANES, SUBLANES, LRU_W)
    sub = lax.broadcasted_iota(jnp.int32, a.shape, 1)
    for d in (1, 2, 4):
        keep = sub >= d
        a_sh = pltpu.roll(a, d, 1)
        b_sh = pltpu.roll(b, d, 1)
        b = jnp.where(keep, a * b_sh + b, b)
        a = jnp.where(keep, a * a_sh, a)
    s["abuf"][0:rows, :] = a.reshape(rows, LRU_W)
    s["bbuf"][0:rows, :] = b.reshape(rows, LRU_W)

    def carry_step(j, hprev):
        off = pl.multiple_of(j * SUBLANES, SUBLANES)
        hj = s["abuf"][pl.ds(off, SUBLANES), :] * hprev + s["bbuf"][pl.ds(off, SUBLANES), :]
        s["bbuf"][pl.ds(off, SUBLANES), :] = hj
        return hj[SUBLANES - 1:SUBLANES, :]

    s["hcar"][...] = lax.fori_loop(0, rows // SUBLANES, carry_step, s["hcar"][...])
    lru_out = s["bbuf"][0:rows, :] * jax.nn.gelu(r_gate, approximate=True)
    return conv_out, lru_out


def _route(logits_t):
    n_e, rows = logits_t.shape
    expert = lax.broadcasted_iota(jnp.int32, (n_e, rows), 0).astype(F32)
    masked = logits_t
    vals, sels = [], []
    for _ in range(TOP_K):
        m = jnp.max(masked, axis=0, keepdims=True)
        first = jnp.min(jnp.where(masked == m, expert, float(n_e)), axis=0, keepdims=True)
        sel = expert == first
        vals.append(m)
        sels.append(sel)
        masked = jnp.where(sel, -jnp.inf, masked)
    exps = [jnp.exp(v - vals[0]) for v in vals]
    denom = exps[0] + exps[1] + exps[2] + exps[3]

    chosen = jnp.zeros_like(logits_t)
    for sel in sels:
        chosen = jnp.where(sel, 1.0, chosen)
    earlier = (lax.broadcasted_iota(jnp.int32, (rows, rows), 0) < lax.broadcasted_iota(jnp.int32, (rows, rows), 1))
    rank = _dot(chosen.astype(BF16), jnp.where(earlier, 1.0, 0.0).astype(BF16))
    count = jnp.sum(chosen, axis=1, keepdims=True)
    pieces = jnp.floor((count + (PIECE - 1.0)) * (1.0 / PIECE))
    e_row = lax.broadcasted_iota(jnp.int32, (n_e, n_e), 0)
    e_col = lax.broadcasted_iota(jnp.int32, (n_e, n_e), 1)
    seg_start = _dot(jnp.where(e_col < e_row, 1.0, 0.0).astype(BF16),
                     jnp.broadcast_to(pieces, (n_e, LANES)).astype(BF16))[:, 0:1] * float(PIECE)
    pos = seg_start + rank

    rows_t = [jnp.sum(jnp.where(sels[k], pos, 0.0), axis=0, keepdims=True) for k in range(TOP_K)]
    rows_t += [exps[k] / denom for k in range(TOP_K)]
    meta_t = jnp.concatenate(rows_t, axis=0)
    pieces_row = jnp.sum(jnp.where(e_row == e_col, jnp.broadcast_to(pieces, (n_e, n_e)), 0.0), axis=0, keepdims=True)
    return meta_t, pieces_row


_MIX_PARAMS = ("g_mix", "w_in", "dw_w", "dw_b", "ln_g", "ln_b", "lc_w", "lc_b", "w_ax", "b_ax", "lam",
               "w_out", "g_ffn", "w_r", "b_r")
_MIX_SCRATCH = ("gbuf", "sbuf", "rbuf", "abuf", "bbuf", "hcar")


def _mixer_kernel(x_ref, meta_tok_ref, *refs):
    n_p = len(_MIX_PARAMS)
    p = dict(zip(_MIX_PARAMS, refs[:n_p]))
    h1_ref, u2_ref, meta_ref, meta_t_ref, cnt_ref = refs[n_p:n_p + 5]
    s = dict(zip(_MIX_SCRATCH, refs[n_p + 5:]))

    @pl.when(pl.program_id(1) == 0)
    def _():
        s["gbuf"][0:CONV_HALO, :] = jnp.zeros((CONV_HALO, CONV_W), F32)
        s["rbuf"][0:LRU_HALO, :] = jnp.zeros((LRU_HALO, LRU_W), F32)
        s["hcar"][...] = jnp.zeros((1, LRU_W), F32)
        _mix_rows(meta_tok_ref[...], True, p, s)

    h = x_ref[0]
    conv_out, lru_out = _mix_rows(h, False, p, s)
    w_out = p["w_out"]
    h1 = h + _dot(conv_out.astype(BF16), w_out[0:CONV_W, :]) + _dot(lru_out.astype(BF16), w_out[CONV_W:, :])
    h1_ref[...] = h1
    u2 = _rms(h1, p["g_ffn"][...]).astype(BF16)
    u2_ref[...] = u2
    logits_t = lax.dot_general(p["w_r"][...], u2, (((1,), (1,)), ((), ())), preferred_element_type=F32) + p["b_r"][...]
    meta_t, pieces = _route(logits_t)
    meta_t_ref[...] = meta_t
    meta_ref[...] = jnp.concatenate([meta_t, jnp.zeros((LANES - meta_t.shape[0], meta_t.shape[1]), F32)], axis=0).T
    cnt_ref[0] = pieces


def _mixer(x, meta_tok, params):
    bsz, seq, d = x.shape
    n_chunks = seq // BLK
    n_tok = bsz * seq

    def const(a):
        return pl.BlockSpec(a.shape, lambda b, c: (0,) * a.ndim)

    row_block = lambda width: pl.BlockSpec((BLK, width), lambda b, c: (b * n_chunks + c, 0))
    return pl.pallas_call(
        _mixer_kernel,
        grid=(bsz, n_chunks),
        in_specs=[pl.BlockSpec((1, BLK, d), lambda b, c: (b, c, 0)), const(meta_tok)]
        + [const(params[k]) for k in _MIX_PARAMS],
        out_specs=[row_block(d), row_block(d), row_block(LANES),
                   pl.BlockSpec((2 * TOP_K, BLK), lambda b, c: (0, b * n_chunks + c)),
                   pl.BlockSpec((1, 1, N_EXPERTS), lambda b, c: (b * n_chunks + c, 0, 0))],
        out_shape=[jax.ShapeDtypeStruct((n_tok, d), F32), jax.ShapeDtypeStruct((n_tok, d), BF16),
                   jax.ShapeDtypeStruct((n_tok, LANES), F32), jax.ShapeDtypeStruct((2 * TOP_K, n_tok), F32),
                   jax.ShapeDtypeStruct((n_tok // BLK, 1, N_EXPERTS), F32)],
        scratch_shapes=[pltpu.VMEM((CONV_HALO + BLK, CONV_W), F32), pltpu.VMEM((CONV_HALO + BLK, CONV_W), F32),
                        pltpu.VMEM((LRU_HALO + BLK, LRU_W), F32),
                        pltpu.VMEM((BLK, LRU_W), F32), pltpu.VMEM((BLK, LRU_W), F32),
                        pltpu.VMEM((1, LRU_W), F32)],
        compiler_params=pltpu.CompilerParams(dimension_semantics=("arbitrary", "arbitrary"),
                                             vmem_limit_bytes=VMEM_LIMIT),
        name="mixer",
    )(x, meta_tok, *[params[k] for k in _MIX_PARAMS])


def _routing_tables(cnt, n_tiles):
    n_blocks, n_e = cnt.shape
    tile_pieces = TILE // PIECE
    per_expert = cnt.sum(0)
    tiles_e = (per_expert + tile_pieces - 1) // tile_pieces
    tile_end = jnp.cumsum(tiles_e)
    tile_start = tile_end - tiles_e
    seg_global = tile_start[None, :] * tile_pieces + jnp.cumsum(cnt, axis=0) - cnt
    seg_end_local = jnp.cumsum(cnt, axis=1)
    seg_local = seg_end_local - cnt
    n_pieces = seg_end_local[:, -1]

    j = jnp.arange(N_PIECES, dtype=jnp.int32)
    shift = seg_global - seg_local
    step = jnp.concatenate([shift[:, :1], shift[:, 1:] - shift[:, :-1]], axis=1)
    dst = j[None, :] + jnp.where(j[None, :, None] >= seg_local[:, None, :], step[:, None, :], 0).sum(-1)
    dst = jnp.where(j[None, :] < n_pieces[:, None], dst * PIECE, 0)

    tail_start = (tile_start * tile_pieces + per_expert) * PIECE
    tail_count = tiles_e * tile_pieces - per_expert

    total = tile_end[-1]
    i = jnp.arange(n_tiles, dtype=jnp.int32)
    valid = i < total
    idx = jnp.where(valid, i, total - 1)
    t_exp = jnp.minimum((idx[:, None] >= tile_end[None, :]).sum(-1), n_e - 1)
    t_first = valid & ((i[:, None] == tile_start[None, :]) & (tiles_e[None, :] > 0)).any(-1)
    left = per_expert[None, :] * PIECE - (i[:, None] - tile_start[None, :]) * TILE
    rows_i = jnp.where(t_exp[:, None] == jnp.arange(n_e)[None, :], jnp.clip(left, 0, TILE), 0).sum(-1)
    t_sub = jnp.where(valid, (rows_i + SUB - 1) // SUB, 0)
    e_ids = jnp.arange(n_e, dtype=jnp.int32)
    later = (e_ids[None, :] > t_exp[:, None]) & (tiles_e[None, :] > 0)
    t_next = jnp.where(later, e_ids[None, :], n_e).min(-1)
    t_next = jnp.where(t_next < n_e, t_next, -1)
    i32 = lambda a: a.astype(jnp.int32)
    misc = jnp.stack([total, tail_count.sum()])
    return (i32(dst).reshape(-1), i32(n_pieces), i32(tail_start), i32(tail_count), i32(misc),
            i32(t_exp), i32(idx), i32(t_sub), i32(t_first), i32(t_next))


def _wait_pieces(count, max_count, hbm_ref, sem):
    bit = 1
    while bit <= max_count:
        @pl.when((count & bit) != 0)
        def _(rows=bit * PIECE):
            pltpu.make_async_copy(hbm_ref.at[0:rows, :], hbm_ref.at[0:rows, :], sem).wait()
        bit *= 2


def _for_each_piece(count, max_count, fn):
    bit = 1 << (max_count.bit_length() - 1)
    base = 0
    while bit:
        @pl.when((count & bit) != 0)
        def _(base=base, bit=bit):
            for q in range(bit):
                fn(base + q)
        base = base + (count & bit)
        bit //= 2


def _dispatch_kernel(dst_ref, npc_ref, tail_start_ref, tail_count_ref, misc_ref, u2_ref, meta_t_ref, xs_hbm,
                     buf, zbuf, sem, zsem, tsem):
    b = pl.program_id(0)
    n_b = pl.num_programs(0)
    slot = b % 2
    n_tiles = xs_hbm.shape[0] // TILE
    used_tiles, n_tail = misc_ref[0], misc_ref[1]

    @pl.when(b >= 2)
    def _():
        _wait_pieces(npc_ref[b - 2], N_PIECES, xs_hbm, sem.at[slot])

    def unused_tile_copy(t):
        return pltpu.make_async_copy(zbuf, xs_hbm.at[pl.ds(pl.multiple_of(t * TILE, TILE), TILE), :], tsem)

    @pl.when(b == 0)
    def _():
        zbuf[...] = jnp.zeros(zbuf.shape, zbuf.dtype)

        def tile_body(t, c):
            unused_tile_copy(t).start()
            return c
        lax.fori_loop(used_tiles, n_tiles, tile_body, 0)

        def expert_body(e, c):
            count = tail_count_ref[e]
            row = tail_start_ref[e]
            bit = 1 << (TAIL_PIECES.bit_length() - 1)
            while bit:
                @pl.when((count & bit) != 0)
                def _(row=row, n=bit * PIECE):
                    pltpu.make_async_copy(zbuf.at[0:n, :], xs_hbm.at[pl.ds(pl.multiple_of(row, PIECE), n), :],
                                          zsem).start()
                row = row + (count & bit) * PIECE
                bit //= 2
            return c
        lax.fori_loop(0, tail_start_ref.shape[0], expert_body, 0)

    pos_t = meta_t_ref[...]
    u = u2_ref[...]

    def sorted_rows(lo, n):
        r = (lax.broadcasted_iota(jnp.int32, (n, BLK), 0) + lo).astype(F32)
        onehot = jnp.zeros((n, BLK), F32)
        for k in range(TOP_K):
            onehot = jnp.where(r == pos_t[k:k + 1, :], 1.0, onehot)
        return _dot(onehot.astype(BF16), u).astype(BF16)

    n_rows = npc_ref[b] * PIECE
    buf[slot, 0:BASE_ROWS, :] = sorted_rows(0, BASE_ROWS)
    for lo in range(BASE_ROWS, SORT_ROWS, CHUNK):
        @pl.when(lo < n_rows)
        def _(lo=lo):
            buf[slot, lo:lo + CHUNK, :] = sorted_rows(lo, CHUNK)

    def issue(j):
        row = pl.multiple_of(dst_ref[b * N_PIECES + j], PIECE)
        src_row = pl.multiple_of(j * PIECE, PIECE)
        pltpu.make_async_copy(buf.at[slot, pl.ds(src_row, PIECE), :], xs_hbm.at[pl.ds(row, PIECE), :],
                              sem.at[slot]).start()
    _for_each_piece(npc_ref[b], N_PIECES, issue)

    @pl.when(b == n_b - 1)
    def _():
        _wait_pieces(npc_ref[b], N_PIECES, xs_hbm, sem.at[slot])

        @pl.when(n_b >= 2)
        def _():
            _wait_pieces(npc_ref[b - 1], N_PIECES, xs_hbm, sem.at[1 - slot])

        _wait_pieces(n_tail, tail_start_ref.shape[0] * TAIL_PIECES, xs_hbm, zsem)

        def tile_body(t, c):
            unused_tile_copy(t).wait()
            return c
        lax.fori_loop(used_tiles, n_tiles, tile_body, 0)


def _dispatch(u2, meta_t, dst, n_pieces, tail_start, tail_count, misc, n_tiles):
    n_tok, d = u2.shape
    n_blocks = n_tok // BLK
    return pl.pallas_call(
        _dispatch_kernel,
        grid_spec=pltpu.PrefetchScalarGridSpec(
            num_scalar_prefetch=5,
            grid=(n_blocks,),
            in_specs=[pl.BlockSpec((BLK, d), lambda b, *_: (b, 0)),
                      pl.BlockSpec((2 * TOP_K, BLK), lambda b, *_: (0, b))],
            out_specs=pl.BlockSpec(memory_space=pl.ANY),
            scratch_shapes=[pltpu.VMEM((2, SORT_ROWS, d), BF16), pltpu.VMEM((TILE, d), BF16),
                            pltpu.SemaphoreType.DMA((2,)), pltpu.SemaphoreType.DMA(()), pltpu.SemaphoreType.DMA(())],
        ),
        out_shape=jax.ShapeDtypeStruct((n_tiles * TILE, d), BF16),
        compiler_params=pltpu.CompilerParams(dimension_semantics=("arbitrary",), vmem_limit_bytes=VMEM_LIMIT),
        name="dispatch",
    )(dst, n_pieces, tail_start, tail_count, misc, u2, meta_t)


def _deinterleave_perm():
    src = lax.broadcasted_iota(jnp.int32, (MXU_DIM, MXU_DIM), 0)
    dst = lax.broadcasted_iota(jnp.int32, (MXU_DIM, MXU_DIM), 1)
    want = jnp.where(dst < LANES, 2 * dst, 2 * (dst - LANES) + 1)
    return jnp.where(src == want, 1.0, 0.0).astype(BF16)


def _load_expert_weights(wgu_ref, wg_s, wl_s):
    perm = _deinterleave_perm()
    d_ff = wg_s.shape[1]
    for c in range(2 * d_ff // MXU_DIM):
        chunk = wgu_ref[:, c * MXU_DIM:(c + 1) * MXU_DIM].astype(BF16)
        o = _dot(chunk, perm)
        wg_s[:, c * LANES:(c + 1) * LANES] = o[:, :LANES].astype(BF16)
        wl_s[:, c * LANES:(c + 1) * LANES] = o[:, LANES:].astype(BF16)


def _expert_mlp(x, wg, wl, bg, bl, wd, bd):
    x_glu = jnp.minimum(_dot(x, wg) + bg, SWIGLU_LIMIT)
    x_lin = jnp.clip(_dot(x, wl) + bl, -SWIGLU_LIMIT, SWIGLU_LIMIT)
    act = x_glu * _sigmoid(SWIGLU_ALPHA * x_glu) * (x_lin + 1.0)
    return _dot(act.astype(BF16), wd) + bd


def _experts_kernel(texp_ref, tblk_ref, tsub_ref, tfirst_ref, tnext_ref, x_ref, bg_ref, bl_ref, bd_ref,
                    wgu_hbm, wd_hbm, y_ref, wgu_f32, wd_f32, wg_s, wl_s, wd_s, sem):
    del tblk_ref
    i = pl.program_id(0)

    def fetch(e):
        return (pltpu.make_async_copy(wgu_hbm.at[e], wgu_f32, sem.at[0]),
                pltpu.make_async_copy(wd_hbm.at[e], wd_f32, sem.at[1]))

    @pl.when(tfirst_ref[i] == 1)
    def _():
        @pl.when(i == 0)
        def _():
            for cp in fetch(texp_ref[0]):
                cp.start()

        for cp in fetch(texp_ref[i]):
            cp.wait()
        _load_expert_weights(wgu_f32, wg_s, wl_s)
        wd_s[...] = wd_f32[...].astype(BF16)

        @pl.when(tnext_ref[i] >= 0)
        def _():
            for cp in fetch(tnext_ref[i]):
                cp.start()

    def mlp_pass(q, c):
        off = pl.multiple_of(q * SUB, SUB)
        y = _expert_mlp(x_ref[pl.ds(off, SUB), :], wg_s[...], wl_s[...], bg_ref[0], bl_ref[0], wd_s[...], bd_ref[0])
        y_ref[pl.ds(off, SUB), :] = y.astype(BF16)
        return c
    lax.fori_loop(0, tsub_ref[i], mlp_pass, 0)

    def zero_pass(q, c):
        y_ref[pl.ds(pl.multiple_of(q * SUB, SUB), SUB), :] = jnp.zeros((SUB, y_ref.shape[1]), y_ref.dtype)
        return c
    lax.fori_loop(tsub_ref[i], TILE // SUB, zero_pass, 0)


def _experts(xs, t_exp, t_blk, t_sub, t_first, t_next, w_gu, b_g, b_l, w_d, b_d):
    n_rows, d = xs.shape
    d_ff = w_gu.shape[2] // 2
    rows = pl.BlockSpec((TILE, d), lambda i, te, tb, *_: (tb[i], 0))
    per_e = lambda a: pl.BlockSpec((1,) + a.shape[1:], lambda i, te, *_: (te[i],) + (0,) * (a.ndim - 1))
    hbm = pl.BlockSpec(memory_space=pl.ANY)
    return pl.pallas_call(
        _experts_kernel,
        grid_spec=pltpu.PrefetchScalarGridSpec(
            num_scalar_prefetch=5,
            grid=(n_rows // TILE,),
            in_specs=[rows, per_e(b_g), per_e(b_l), per_e(b_d), hbm, hbm],
            out_specs=pl.BlockSpec((TILE, d), lambda i, *_: (i, 0)),
            scratch_shapes=[pltpu.VMEM(w_gu.shape[1:], F32), pltpu.VMEM(w_d.shape[1:], F32),
                            pltpu.VMEM((d, d_ff), BF16), pltpu.VMEM((d, d_ff), BF16), pltpu.VMEM((d_ff, d), BF16),
                            pltpu.SemaphoreType.DMA((2,))],
        ),
        out_shape=jax.ShapeDtypeStruct((n_rows, d), BF16),
        compiler_params=pltpu.CompilerParams(dimension_semantics=("arbitrary",), vmem_limit_bytes=VMEM_LIMIT),
        name="experts",
    )(t_exp, t_blk, t_sub, t_first, t_next, xs, b_g, b_l, b_d, w_gu, w_d)


def _combine_kernel(dst_ref, npc_ref, meta_ref, h1_ref, gfin_ref, ys_hbm, out_ref, ybuf, acc_ref, sem):
    b = pl.program_id(0)
    n_b = pl.num_programs(0)
    slot = b % 2

    def start_piece(blk, slot_, j):
        row = pl.multiple_of(dst_ref[blk * N_PIECES + j], PIECE)
        dst_row = pl.multiple_of(j * PIECE, PIECE)
        pltpu.make_async_copy(ys_hbm.at[pl.ds(row, PIECE), :], ybuf.at[slot_, pl.ds(dst_row, PIECE), :],
                              sem.at[slot_]).start()

    @pl.when(b == 0)
    def _():
        ybuf[...] = jnp.zeros(ybuf.shape, ybuf.dtype)

        def body(j, c):
            start_piece(0, 0, j)
            return c
        lax.fori_loop(0, npc_ref[0], body, 0)

    @pl.when(b + 1 < n_b)
    def _():
        _for_each_piece(npc_ref[b + 1], N_PIECES, lambda j: start_piece(b + 1, 1 - slot, j))

    meta = meta_ref[...]

    def gate_weights(lo, n):
        r = lax.broadcasted_iota(jnp.int32, (BLK, n), 1).astype(F32)
        w = jnp.zeros((BLK, n), F32)
        for k in range(TOP_K):
            w = jnp.where(r == meta[:, k:k + 1] - float(lo), meta[:, TOP_K + k:TOP_K + k + 1], w)
        return w.astype(BF16)

    w_base = gate_weights(0, BASE_ROWS)
    _wait_pieces(npc_ref[b], N_PIECES, ys_hbm, sem.at[slot])
    acc_ref[...] = h1_ref[...] + _dot(w_base, ybuf[slot, 0:BASE_ROWS, :])
    n_rows = npc_ref[b] * PIECE
    for lo in range(BASE_ROWS, SORT_ROWS, CHUNK):
        @pl.when(lo < n_rows)
        def _(lo=lo):
            acc_ref[...] += _dot(gate_weights(lo, CHUNK), ybuf[slot, lo:lo + CHUNK, :])

    out_ref[...] = _rms(acc_ref[...], gfin_ref[...])


def _combine(ys, meta, h1, g_final, dst, n_pieces):
    n_tok, d = h1.shape
    n_blocks = n_tok // BLK
    return pl.pallas_call(
        _combine_kernel,
        grid_spec=pltpu.PrefetchScalarGridSpec(
            num_scalar_prefetch=2,
            grid=(n_blocks,),
            in_specs=[pl.BlockSpec((BLK, LANES), lambda b, *_: (b, 0)), pl.BlockSpec((BLK, d), lambda b, *_: (b, 0)),
                      pl.BlockSpec(g_final.shape, lambda b, *_: (0, 0)), pl.BlockSpec(memory_space=pl.ANY)],
            out_specs=pl.BlockSpec((BLK, d), lambda b, *_: (b, 0)),
            scratch_shapes=[pltpu.VMEM((2, SORT_ROWS, d), BF16), pltpu.VMEM((BLK, d), F32),
                            pltpu.SemaphoreType.DMA((2,))],
        ),
        out_shape=jax.ShapeDtypeStruct((n_tok, d), F32),
        compiler_params=pltpu.CompilerParams(dimension_semantics=("arbitrary",), vmem_limit_bytes=VMEM_LIMIT),
        name="combine",
    )(dst, n_pieces, meta, h1, g_final, ys)


def _block_diag(w):
    heads, n, _ = w.shape
    eye = jnp.eye(heads, dtype=w.dtype)
    return (eye[:, None, :, None] * w[:, :, None, :]).reshape(heads * n, heads * n)


def kernel(x, meta_tokens, g_mix, w_in, conv_dw_w, conv_dw_b, conv_ln_g, conv_ln_b, lru_conv_w, lru_conv_b,
           lru_wa, lru_ba, lru_wx, lru_bx, lru_lambda, w_out, g_ffn, w_router, b_router, w_gate_up, b_gate_up,
           w_down, b_down, g_final):
    bsz, seq, d = x.shape
    n_e = w_gate_up.shape[1]
    assert w_in.shape[0] == 1 and seq % BLK == 0 and n_e == N_EXPERTS
    n_blocks = bsz * seq // BLK
    row = lambda v: v.reshape(1, -1)
    params = {
        "g_mix": row(g_mix[0]), "w_in": w_in[0].astype(BF16),
        "dw_w": conv_dw_w[0], "dw_b": row(conv_dw_b[0]), "ln_g": row(conv_ln_g[0]), "ln_b": row(conv_ln_b[0]),
        "lc_w": lru_conv_w[0], "lc_b": row(lru_conv_b[0]),
        "w_ax": jnp.concatenate([_block_diag(lru_wa[0]), _block_diag(lru_wx[0])], axis=1).astype(BF16),
        "b_ax": row(jnp.concatenate([lru_ba[0], lru_bx[0]])), "lam": row(lru_lambda[0]),
        "w_out": w_out[0].astype(BF16), "g_ffn": row(g_ffn[0]),
        "w_r": w_router[0].T.astype(BF16), "b_r": b_router[0].reshape(-1, 1),
    }
    h1, u2, meta, meta_t, cnt = _mixer(x, meta_tokens, params)

    n_tiles = -(-(n_blocks * (TOP_K * BLK + n_e * (PIECE - 1))) // TILE) + n_e
    (dst, n_pieces, tail_start, tail_count, misc, t_exp, t_blk, t_sub, t_first,
     t_next) = _routing_tables(cnt[:, 0, :].astype(jnp.int32), n_tiles)
    xs = _dispatch(u2, meta_t, dst, n_pieces, tail_start, tail_count, misc, n_tiles)
    b_gu = b_gate_up[0].reshape(n_e, 1, -1, 2)
    ys = _experts(xs, t_exp, t_blk, t_sub, t_first, t_next, w_gate_up.reshape(w_gate_up.shape[1:]),
                  b_gu[..., 0], b_gu[..., 1], w_down.reshape(w_down.shape[1:]), b_down[0].reshape(n_e, 1, -1))
    out = _combine(ys, meta, h1, row(g_final), dst, n_pieces)
    return out.reshape(bsz, seq, d)
```
